```python
import math
import jax, jax.numpy as jnp
from jax import lax
import numpy as np

D_MODEL = 4096
BATCH = 4
SEQ = 2048
DEPTH = 2

N_A_LAYERS = DEPTH // 2
N_B_LAYERS = DEPTH - N_A_LAYERS
N_DENSE_LAYERS = (DEPTH + 1) // 2
N_MOE_LAYERS = DEPTH // 2

HEAD_DIM = 128
N_HEADS = D_MODEL // HEAD_DIM
ATTN_WIDTH = N_HEADS * HEAD_DIM
ROPE_THETA = 10000.0
DILATED_BRANCHES = ((128, 1), (512, 4), (2048, 16))
ATTN_BLOCK = 128

POOL_WINDOWS = (2, 4, 8, 16)
POOL_GROUPS = len(POOL_WINDOWS)
POOL_GROUP_DIM = D_MODEL // POOL_GROUPS

D_FF = 11008 * D_MODEL // 4096
N_EXPERTS = 8
TOP_K = 2
D_FF_EXPERT = 7 * D_MODEL // 8
PLE_DIM = 256
RMS_EPS = 1e-6

kernel_name = "yoco_pool_dilated_moe_hybrid"

F32 = jnp.float32


def rms_norm(x, g):
    xf = x.astype(F32)
    y = xf * lax.rsqrt(jnp.mean(xf * xf, axis=-1, keepdims=True) + RMS_EPS)
    return (y * g.astype(F32)).astype(x.dtype)


def rope(x):
    s = x.shape[1]
    half = HEAD_DIM // 2
    inv_freq = jnp.exp(-math.log(ROPE_THETA) * jnp.arange(half, dtype=F32) / half)
    ang = jnp.arange(s, dtype=F32)[:, None] * inv_freq[None, :]
    cos = jnp.cos(ang)[None, :, None, :]
    sin = jnp.sin(ang)[None, :, None, :]
    xf = x.astype(F32)
    x1, x2 = xf[..., :half], xf[..., half:]
    return jnp.concatenate([x1 * cos - x2 * sin, x2 * cos + x1 * sin], axis=-1).astype(x.dtype)


def multiscale_pool(xn, w_groups, scale):
    b, s, d = xn.shape
    xf = xn.astype(F32)
    cs = jnp.concatenate([jnp.zeros((b, 1, d), F32), jnp.cumsum(xf, axis=1)], axis=1)
    t = jnp.arange(s)
    outs = []
    for g, w in enumerate(POOL_WINDOWS):
        lo_c, hi_c = g * POOL_GROUP_DIM, (g + 1) * POOL_GROUP_DIM
        csg = cs[:, :, lo_c:hi_c]
        lo = jnp.maximum(t + 1 - w, 0)
        win_sum = csg[:, 1:, :] - csg[:, lo, :]
        cnt = jnp.minimum(t + 1, w).astype(F32)[None, :, None]
        outs.append(win_sum / cnt - xf[:, :, lo_c:hi_c])
    pooled = jnp.stack(outs, axis=2).astype(xn.dtype)
    mixed = jnp.einsum('bsgc,gce->bsge', pooled, w_groups).reshape(b, s, d)
    return mixed * scale


def swiglu(x, w_gate_up, w_down):
    g, u = jnp.split(x @ w_gate_up, 2, axis=-1)
    return (jax.nn.silu(g) * u) @ w_down


def moe_swiglu(xn, router, w_gate_up, w_down):
    b, s, d = xn.shape
    xt = xn.reshape(-1, d)
    logits = (xt @ router).astype(F32)
    top_logit, top_idx = lax.top_k(logits, TOP_K)
    gates = jax.nn.softmax(top_logit, axis=-1)
    combine = jnp.sum(jax.nn.one_hot(top_idx, N_EXPERTS, dtype=F32) * gates[..., None], axis=1)
    out = jnp.zeros(xt.shape, F32)
    for e in range(N_EXPERTS):
        out = out + combine[:, e:e + 1] * swiglu(xt, w_gate_up[e], w_down[e]).astype(F32)
    return out.astype(xn.dtype).reshape(b, s, d)


def per_layer_embedding(h, p_i, w_ple, ple_norm, w_ple_gate):
    gate = jax.nn.sigmoid((rms_norm(h, ple_norm) @ w_ple_gate).astype(F32))
    return h + ((p_i @ w_ple).astype(F32) * gate).astype(h.dtype)


def shared_kv(h, kv_norm, w_kv, k_norm):
    b, s, _ = h.shape
    kv = rms_norm(h, kv_norm) @ w_kv
    k, v = jnp.split(kv, 2, axis=-1)
    k = rope(rms_norm(k.reshape(b, s, N_HEADS, HEAD_DIM), k_norm)).astype(F32)
    v = v.reshape(b, s, N_HEADS, HEAD_DIM).astype(F32)
    return k, v


def dilated_branch(q, k, v, window, dilation):
    b, s, h, hd = q.shape
    L = s // dilation
    n_keys = window // dilation
    blk = ATTN_BLOCK
    nb = -(-L // blk)
    Lp = nb * blk
    n = b * dilation

    def to_strided(t):
        t = t.reshape(b, L, dilation, h, hd).transpose(0, 2, 1, 3, 4).reshape(n, L, h, hd)
        return jnp.pad(t, ((0, 0), (0, Lp - L), (0, 0), (0, 0)))

    def band(t):
        tp = jnp.pad(t, ((0, 0), (blk, 0), (0, 0), (0, 0))).reshape(n, nb + 1, blk, h, hd)
        return jnp.concatenate([tp[:, :-1], tp[:, 1:]], axis=2)

    qb = to_strided(q).reshape(n, nb, blk, h, hd)
    kb = band(to_strided(k))
    vb = band(to_strided(v))

    qi = jnp.arange(blk)[:, None]
    kj = jnp.arange(2 * blk)[None, :]
    dist = qi + blk - kj
    kpos = (jnp.arange(nb)[:, None, None] - 1) * blk + kj[None]
    mask = (dist >= 0)[None] & (dist <= n_keys)[None] & (kpos >= 0)

    scores = jnp.einsum('nbqhd,nbkhd->nbhqk', qb, kb) / math.sqrt(hd)
    scores = jnp.where(mask[None, :, None], scores, -jnp.inf)
    m = jnp.max(scores, axis=-1, keepdims=True)
    e = jnp.exp(scores - m)
    den = jnp.sum(e, axis=-1, keepdims=True)
    o = jnp.einsum('nbhqk,nbkhd->nbqhd', e / den, vb)
    lse = (m + jnp.log(den))[..., 0].transpose(0, 1, 3, 2)

    def from_strided(t):
        rest = t.shape[2:]
        t = t[:, :L].reshape((b, dilation, L) + rest)
        t = jnp.moveaxis(t, 1, 2)
        return t.reshape((b, s) + rest)

    return from_strided(o.reshape(n, Lp, h, hd)), from_strided(lse.reshape(n, Lp, h))


def dilated_attention(xn, k, v, w_q, q_norm, w_o):
    b, s, _ = xn.shape
    q = (xn @ w_q).reshape(b, s, N_HEADS, HEAD_DIM)
    q = rope(rms_norm(q, q_norm)).astype(F32)
    outs, lses = [], []
    for window, dilation in DILATED_BRANCHES:
        o_i, lse_i = dilated_branch(q, k, v, window, dilation)
        outs.append(o_i)
        lses.append(lse_i)
    alpha = jax.nn.softmax(jnp.stack(lses, axis=0), axis=0)[..., None]
    o = jnp.sum(alpha * jnp.stack(outs, axis=0), axis=0)
    return o.reshape(b, s, ATTN_WIDTH).astype(xn.dtype) @ w_o


def setup_inputs(seed: int = 0) -> dict:
    key = jax.random.key(seed)
    ks = jax.random.split(key, 24)

    def w(k, shape, fan_in):
        return jax.random.normal(k, shape, F32) * (fan_in ** -0.5)

    def gain(k, shape, noise=0.02):
        return jnp.ones(shape, F32) + noise * jax.random.normal(k, shape, F32)

    return {
        "x": jax.random.normal(ks[0], (BATCH, SEQ, D_MODEL), F32),
        "p": jax.random.normal(ks[1], (DEPTH, BATCH, SEQ, PLE_DIM), F32),
        "pool_norm": gain(ks[2], (N_A_LAYERS, D_MODEL)),
        "pool_w": w(ks[3], (N_A_LAYERS, POOL_GROUPS, POOL_GROUP_DIM, POOL_GROUP_DIM), POOL_GROUP_DIM),
        "pool_scale": gain(ks[4], (N_A_LAYERS, D_MODEL), 0.1),
        "kv_norm": gain(ks[5], (D_MODEL,)),
        "w_kv": w(ks[6], (D_MODEL, 2 * ATTN_WIDTH), D_MODEL),
        "k_norm": gain(ks[7], (HEAD_DIM,)),
        "attn_norm": gain(ks[8], (N_B_LAYERS, D_MODEL)),
        "w_q": w(ks[9], (N_B_LAYERS, D_MODEL, ATTN_WIDTH), D_MODEL),
        "q_norm": gain(ks[10], (N_B_LAYERS, HEAD_DIM)),
        "w_o": w(ks[11], (N_B_LAYERS, ATTN_WIDTH, D_MODEL), ATTN_WIDTH),
        "ffn_norm": gain(ks[12], (DEPTH, D_MODEL)),
        "dense_w_gate_up": w(ks[13], (N_DENSE_LAYERS, D_MODEL, 2 * D_FF), D_MODEL),
        "dense_w_down": w(ks[14], (N_DENSE_LAYERS, D_FF, D_MODEL), D_FF),
        "moe_router": w(ks[15], (N_MOE_LAYERS, D_MODEL, N_EXPERTS), D_MODEL),
        "moe_w_gate_up": w(ks[16], (N_MOE_LAYERS, N_EXPERTS, D_MODEL, 2 * D_FF_EXPERT), D_MODEL),
        "moe_w_down": w(ks[17], (N_MOE_LAYERS, N_EXPERTS, D_FF_EXPERT, D_MODEL), D_FF_EXPERT),
        "ple_w": w(ks[18], (DEPTH, PLE_DIM, D_MODEL), PLE_DIM),
        "ple_norm": gain(ks[19], (DEPTH, D_MODEL)),
        "ple_gate_w": w(ks[20], (DEPTH, D_MODEL, D_MODEL), D_MODEL),
    }


def reference(x, p, pool_norm, pool_w, pool_scale, kv_norm, w_kv, k_norm, attn_norm, w_q,
              q_norm, w_o, ffn_norm, dense_w_gate_up, dense_w_down, moe_router,
              moe_w_gate_up, moe_w_down, ple_w, ple_norm, ple_gate_w):
    h = x
    k_shared = None
    v_shared = None
    for i in range(DEPTH):
        if i < N_A_LAYERS:
            a = i
            h = h + multiscale_pool(rms_norm(h, pool_norm[a]), pool_w[a], pool_scale[a])
        else:
            bi = i - N_A_LAYERS
            h = h + dilated_attention(rms_norm(h, attn_norm[bi]), k_shared, v_shared,
                                      w_q[bi], q_norm[bi], w_o[bi])
        hn = rms_norm(h, ffn_norm[i])
        if i % 2 == 0:
            h = h + swiglu(hn, dense_w_gate_up[i // 2], dense_w_down[i // 2])
        else:
            h = h + moe_swiglu(hn, moe_router[i // 2], moe_w_gate_up[i // 2], moe_w_down[i // 2])
        h = per_layer_embedding(h, p[i], ple_w[i], ple_norm[i], ple_gate_w[i])
        if i == N_A_LAYERS - 1:
            k_shared, v_shared = shared_kv(h, kv_norm, w_kv, k_norm)
    return h
```

```python
import functools
import math

import jax
import jax.numpy as jnp
from jax import lax
from jax.experimental import pallas as pl
from jax.experimental.pallas import tpu as pltpu

F32 = jnp.float32
BF16 = jnp.bfloat16

HEAD_DIM = 128
LANES = 128
ROPE_THETA = 10000.0
RMS_EPS = 1e-6
POOL_WINDOWS = (2, 4, 8, 16)
POOL_HALO = 16
DILATED_BRANCHES = ((128, 1), (512, 4), (2048, 16))
ATTN_BLOCK = 128
N_EXPERTS = 8
TOP_K = 2
NEG = -1e30
MIB = 1024 * 1024


def _params(semantics, vmem_mib):
    return pltpu.CompilerParams(dimension_semantics=semantics, vmem_limit_bytes=vmem_mib * MIB)


def _tile(dim, pref, quantum):
    if dim <= pref:
        return dim
    t = (pref // quantum) * quantum
    while t > quantum and dim % t:
        t -= quantum
    assert dim % t == 0, (dim, pref, quantum)
    return t


def _rms_scale(x):
    return lax.rsqrt(jnp.mean(x * x, axis=-1, keepdims=True) + RMS_EPS)


def _pool_body(x_ref, halo_ref, xcol_ref, w_ref, gain_ref, scale_ref, next_gain_ref,
               h_ref, hn_ref, yn_ref, hrow_ref, *, ts, seq, n_groups, gdim):
    i = pl.program_id(0)
    g = pl.program_id(1)
    t0 = (i * ts) % seq

    @pl.when(g == 0)
    def _():
        x = x_ref[...]
        y = x * _rms_scale(x) * gain_ref[...]
        xh = halo_ref[...]
        yh = xh * _rms_scale(xh) * gain_ref[...]
        yh = yh * (t0 > 0).astype(F32)
        for gi in range(n_groups):
            yn_ref[gi, :POOL_HALO, :] = yh[:, gi * gdim:(gi + 1) * gdim]
            yn_ref[gi, POOL_HALO:, :] = y[:, gi * gdim:(gi + 1) * gdim]

    pos = t0 + lax.broadcasted_iota(jnp.int32, (ts, 1), 0)
    for gi, win in enumerate(POOL_WINDOWS):
        @pl.when(g == gi)
        def _():
            cur = yn_ref[gi, POOL_HALO:, :]
            acc = cur
            for j in range(1, win):
                acc = acc + yn_ref[gi, POOL_HALO - j:POOL_HALO - j + ts, :]
            inv_cnt = 1.0 / jnp.minimum(pos + 1, win).astype(F32)
            pooled = (acc * inv_cnt - cur).astype(BF16)
            mixed = jnp.dot(pooled, w_ref[...].astype(BF16), preferred_element_type=F32)
            hrow_ref[gi] = xcol_ref[...] + mixed * scale_ref[...]

    h_ref[...] = hrow_ref[g]

    @pl.when(g == n_groups - 1)
    def _():
        ss = jnp.zeros((ts, 1), F32)
        for gi in range(n_groups):
            hg = hrow_ref[gi]
            ss = ss + jnp.sum(hg * hg, axis=-1, keepdims=True)
        inv = lax.rsqrt(ss / (n_groups * gdim) + RMS_EPS)
        for gi in range(n_groups):
            sl = slice(gi * gdim, (gi + 1) * gdim)
            hn_ref[:, sl] = (hrow_ref[gi] * inv * next_gain_ref[:, sl]).astype(BF16)


def _pool_layer(x2, seq, norm_gain, w_groups, scale, next_gain):
    m, d = x2.shape
    n_groups, gdim = w_groups.shape[0], w_groups.shape[1]
    assert n_groups == len(POOL_WINDOWS) and n_groups * gdim == d
    ts = _tile(seq, 256, POOL_HALO)
    hpt = ts // POOL_HALO
    body = functools.partial(_pool_body, ts=ts, seq=seq, n_groups=n_groups, gdim=gdim)
    return pl.pallas_call(
        body,
        grid=(m // ts, n_groups),
        in_specs=[
            pl.BlockSpec((ts, d), lambda i, g: (i, 0)),
            pl.BlockSpec((POOL_HALO, d), lambda i, g: (jnp.maximum(i * hpt - 1, 0), 0)),
            pl.BlockSpec((ts, gdim), lambda i, g: (i, g)),
            pl.BlockSpec((None, gdim, gdim), lambda i, g: (g, 0, 0)),
            pl.BlockSpec((1, d), lambda i, g: (0, 0)),
            pl.BlockSpec((1, gdim), lambda i, g: (0, g)),
            pl.BlockSpec((1, d), lambda i, g: (0, 0)),
        ],
        out_specs=[
            pl.BlockSpec((ts, gdim), lambda i, g: (i, g)),
            pl.BlockSpec((ts, d), lambda i, g: (i, 0)),
        ],
        out_shape=[jax.ShapeDtypeStruct((m, d), F32), jax.ShapeDtypeStruct((m, d), BF16)],
        scratch_shapes=[
            pltpu.VMEM((n_groups, ts + POOL_HALO, gdim), F32),
            pltpu.VMEM((n_groups, ts, gdim), F32),
        ],
        compiler_params=_params(("arbitrary", "arbitrary"), 48),
        name="pool_layer",
    )(x2, x2, x2, w_groups, norm_gain.reshape(1, d), scale.reshape(1, d), next_gain.reshape(1, d))


def _rms_body(x_ref, *refs, n_out):
    gains, outs = refs[:n_out], refs[n_out:]
    x = x_ref[...]
    y = x * _rms_scale(x)
    for g_ref, o_ref in zip(gains, outs):
        o_ref[...] = (y * g_ref[...]).astype(BF16)


def _rms_cast(x2, gains):
    m, d = x2.shape
    tm = _tile(m, 256, 8)
    n_out = len(gains)
    row = pl.BlockSpec((tm, d), lambda i: (i, 0))
    vec = pl.BlockSpec((1, d), lambda i: (0, 0))
    return pl.pallas_call(
        functools.partial(_rms_body, n_out=n_out),
        grid=(m // tm,),
        in_specs=[row] + [vec] * n_out,
        out_specs=[row] * n_out,
        out_shape=[jax.ShapeDtypeStruct((m, d), BF16)] * n_out,
        compiler_params=_params(("arbitrary",), 32),
        name="rms_cast",
    )(x2, *[g.reshape(1, d) for g in gains])


def _ws_matmul(x, ws, extras, *, grid, x_spec, w_specs, extra_specs, out_shape, out_specs,
               epilogue, tn, cast, prefetch=(), recast=None, valid=None, extra_scratch=(),
               vmem_mib=48, name="ws_matmul"):
    n_w, n_e, n_pf = len(ws), len(extras), len(prefetch)
    outs = out_shape if isinstance(out_shape, (list, tuple)) else [out_shape]
    n_out = len(outs)
    k = x.shape[1]
    cast_rows = _tile(k, 512, 8)

    def body(*refs):
        pf, refs = refs[:n_pf], refs[n_pf:]
        x_ref = refs[0]
        w_refs = refs[1:1 + n_w]
        e_refs = refs[1 + n_w:1 + n_w + n_e]
        o_refs = refs[1 + n_w + n_e:1 + n_w + n_e + n_out]
        scratch = refs[1 + n_w + n_e + n_out:]
        n, m = pl.program_id(0), pl.program_id(1)

        def compute():
            if cast:
                wb_ref = scratch[0]

                @pl.when(recast(pf, n, m))
                def _():
                    def chunk(c, carry):
                        r0 = pl.multiple_of(c * cast_rows, cast_rows)
                        for j, w_ref in enumerate(w_refs):
                            wb_ref[pl.ds(r0, cast_rows), j * tn:(j + 1) * tn] = (
                                w_ref[pl.ds(r0, cast_rows), :].astype(BF16))
                        return carry
                    lax.fori_loop(0, k // cast_rows, chunk, 0)
                rhs = wb_ref[...]
            else:
                rhs = w_refs[0][...]
            acc = jnp.dot(x_ref[...], rhs, preferred_element_type=F32)
            epilogue(acc, e_refs, o_refs, scratch[1:] if cast else scratch, pf, n, m)

        if valid is None:
            compute()
        else:
            ok = valid(pf, n, m)
            pl.when(ok)(compute)

            @pl.when(jnp.logical_not(ok))
            def _():
                for o_ref in o_refs:
                    o_ref[...] = jnp.zeros(o_ref.shape, o_ref.dtype)

    scratch_shapes = ([pltpu.VMEM((k, n_w * tn), BF16)] if cast else []) + list(extra_scratch)
    return pl.pallas_call(
        body,
        grid_spec=pltpu.PrefetchScalarGridSpec(
            num_scalar_prefetch=n_pf,
            grid=grid,
            in_specs=[x_spec] + list(w_specs) + list(extra_specs),
            out_specs=out_specs,
            scratch_shapes=scratch_shapes,
        ),
        out_shape=out_shape,
        compiler_params=_params(("arbitrary", "arbitrary"), vmem_mib),
        name=name,
    )(*prefetch, x, *ws, *extras)


def _first_row_tile(pf, n, m):
    return m == 0


def _silu_mul(acc, tn):
    gate, up = acc[:, :tn], acc[:, tn:]
    return gate * jax.nn.sigmoid(gate) * up


def _gate_up(xn, w_gate_up):
    m, k = xn.shape
    dff = w_gate_up.shape[1] // 2
    tm, tn = _tile(m, 1024, 8), _tile(dff, 256, LANES)
    nt = dff // tn

    def epilogue(acc, e_refs, o_refs, scratch, pf, n, mi):
        o_refs[0][...] = _silu_mul(acc, tn).astype(BF16)

    return _ws_matmul(
        xn, [w_gate_up, w_gate_up], [],
        grid=(nt, m // tm),
        x_spec=pl.BlockSpec((tm, k), lambda n, mi: (mi, 0)),
        w_specs=[pl.BlockSpec((k, tn), lambda n, mi: (0, n)),
                 pl.BlockSpec((k, tn), lambda n, mi: (0, n + nt))],
        extra_specs=[],
        out_shape=jax.ShapeDtypeStruct((m, dff), BF16),
        out_specs=pl.BlockSpec((tm, tn), lambda n, mi: (mi, n)),
        epilogue=epilogue, tn=tn, cast=True, recast=_first_row_tile, name="dense_gate_up")


def _matmul_residual(x, w, res, *, name):
    m, k = x.shape
    n_out = w.shape[1]
    cast = w.dtype != BF16
    tm, tn = _tile(m, 512, 8), _tile(n_out, 512, LANES)

    def epilogue(acc, e_refs, o_refs, scratch, pf, n, mi):
        o_refs[0][...] = e_refs[0][...] + acc

    return _ws_matmul(
        x, [w], [res],
        grid=(n_out // tn, m // tm),
        x_spec=pl.BlockSpec((tm, k), lambda n, mi: (mi, 0)),
        w_specs=[pl.BlockSpec((k, tn), lambda n, mi: (0, n))],
        extra_specs=[pl.BlockSpec((tm, tn), lambda n, mi: (mi, n))],
        out_shape=jax.ShapeDtypeStruct((m, n_out), F32),
        out_specs=pl.BlockSpec((tm, tn), lambda n, mi: (mi, n)),
        epilogue=epilogue, tn=tn, cast=cast, recast=_first_row_tile, vmem_mib=56, name=name)


def _ple(hn, w_gate, p_bf16, w_ple, h):
    m, k = hn.shape
    d = w_gate.shape[1]
    pdim = p_bf16.shape[1]
    tm, tn = _tile(m, 1024, 8), _tile(d, 512, LANES)

    def epilogue(acc, e_refs, o_refs, scratch, pf, n, mi):
        p_ref, wp_ref, h_ref = e_refs
        emb = jnp.dot(p_ref[...], wp_ref[...].astype(BF16), preferred_element_type=F32)
        o_refs[0][...] = h_ref[...] + emb * jax.nn.sigmoid(acc)

    return _ws_matmul(
        hn, [w_gate], [p_bf16, w_ple, h],
        grid=(d // tn, m // tm),
        x_spec=pl.BlockSpec((tm, k), lambda n, mi: (mi, 0)),
        w_specs=[pl.BlockSpec((k, tn), lambda n, mi: (0, n))],
        extra_specs=[pl.BlockSpec((tm, pdim), lambda n, mi: (mi, 0)),
                     pl.BlockSpec((pdim, tn), lambda n, mi: (0, n)),
                     pl.BlockSpec((tm, tn), lambda n, mi: (mi, n))],
        out_shape=jax.ShapeDtypeStruct((m, d), F32),
        out_specs=pl.BlockSpec((tm, tn), lambda n, mi: (mi, n)),
        epilogue=epilogue, tn=tn, cast=True, recast=_first_row_tile, vmem_mib=56, name="ple")


def _rope_tables(seq):
    half = HEAD_DIM // 2
    inv_freq = jnp.exp(-math.log(ROPE_THETA) * jnp.arange(half, dtype=F32) / half)
    ang = jnp.arange(seq, dtype=F32)[:, None] * inv_freq[None, :]
    cos, sin = jnp.cos(ang), jnp.sin(ang)
    return jnp.concatenate([cos, cos], axis=-1), jnp.concatenate([-sin, sin], axis=-1)


def _norm_rope_heads(acc, gain, cos, sin_signed, out_ref, tn, post_scale):
    for hh in range(tn // HEAD_DIM):
        sl = slice(hh * HEAD_DIM, (hh + 1) * HEAD_DIM)
        y = acc[:, sl]
        y = y * _rms_scale(y) * gain
        y = y * cos + pltpu.roll(y, HEAD_DIM // 2, axis=1) * sin_signed
        if post_scale != 1.0:
            y = y * post_scale
        out_ref[:, sl] = y.astype(BF16)


def _qk_projection(xn, w, head_gain, cos, sin_signed, seq, *, n_rope_cols, post_scale, name):
    m, k = xn.shape
    n_out = w.shape[1]
    tm, tn = _tile(seq, 1024, 8), _tile(min(n_out, n_rope_cols), 512, LANES)
    n_rope_tiles = n_rope_cols // tn
    spb = seq // tm

    def epilogue(acc, e_refs, o_refs, scratch, pf, n, mi):
        g_ref, c_ref, s_ref = e_refs

        @pl.when(n < n_rope_tiles)
        def _():
            _norm_rope_heads(acc, g_ref[...], c_ref[...], s_ref[...], o_refs[0], tn, post_scale)

        if n_rope_tiles * tn < n_out:
            @pl.when(n >= n_rope_tiles)
            def _():
                o_refs[0][...] = acc.astype(BF16)

    return _ws_matmul(
        xn, [w], [head_gain.reshape(1, HEAD_DIM), cos, sin_signed],
        grid=(n_out // tn, m // tm),
        x_spec=pl.BlockSpec((tm, k), lambda n, mi: (mi, 0)),
        w_specs=[pl.BlockSpec((k, tn), lambda n, mi: (0, n))],
        extra_specs=[pl.BlockSpec((1, HEAD_DIM), lambda n, mi: (0, 0)),
                     pl.BlockSpec((tm, HEAD_DIM), lambda n, mi: (mi % spb, 0)),
                     pl.BlockSpec((tm, HEAD_DIM), lambda n, mi: (mi % spb, 0))],
        out_shape=jax.ShapeDtypeStruct((m, n_out), BF16),
        out_specs=pl.BlockSpec((tm, tn), lambda n, mi: (mi, n)),
        epilogue=epilogue, tn=tn, cast=True, recast=_first_row_tile, vmem_mib=56, name=name)


def _attn_body(*refs, n_heads, has_prev_block, has_state, last):
    refs = list(refs)
    q_ref, kc_ref, vc_ref = refs[:3]
    refs = refs[3:]
    if has_prev_block:
        kp_ref, vp_ref = refs[:2]
        refs = refs[2:]
    if has_state:
        o_in_ref, lse_in_ref = refs[:2]
        refs = refs[2:]
    o_ref = refs[0]
    lse_ref = None if last else refs[1]

    qb = pl.program_id(2)
    blk = ATTN_BLOCK
    row = lax.broadcasted_iota(jnp.int32, (blk, blk), 0)
    col = lax.broadcasted_iota(jnp.int32, (blk, blk), 1)
    bias_cur = jnp.where(col <= row, 0.0, NEG).astype(F32)
    if has_prev_block:
        bias_prev = jnp.where((col >= row) & (qb > 0), 0.0, NEG).astype(F32)
    lane = lax.broadcasted_iota(jnp.int32, (blk, LANES), 1)
    lse_prev_tile = lse_in_ref[...] if has_state else None
    contract_last = (((1,), (1,)), ((), ()))

    def head(h, lse_tile):
        off = pl.multiple_of(h * HEAD_DIM, HEAD_DIM)
        hs = pl.ds(off, HEAD_DIM)
        q = q_ref[:, hs]
        s_c = lax.dot_general(q, kc_ref[:, hs], contract_last, preferred_element_type=F32) + bias_cur
        mx = jnp.max(s_c, axis=-1, keepdims=True)
        if has_prev_block:
            s_p = lax.dot_general(q, kp_ref[:, hs], contract_last, preferred_element_type=F32) + bias_prev
            mx = jnp.maximum(mx, jnp.max(s_p, axis=-1, keepdims=True))
        p_c = jnp.exp(s_c - mx)
        den = jnp.sum(p_c, axis=-1, keepdims=True)
        o = jnp.dot(p_c.astype(BF16), vc_ref[:, hs], preferred_element_type=F32)
        if has_prev_block:
            p_p = jnp.exp(s_p - mx)
            den = den + jnp.sum(p_p, axis=-1, keepdims=True)
            o = o + jnp.dot(p_p.astype(BF16), vp_ref[:, hs], preferred_element_type=F32)
        o = o / den
        lse = mx + jnp.log(den)
        if has_state:
            lse_old = jnp.sum(jnp.where(lane == h, lse_prev_tile, 0.0), axis=-1, keepdims=True)
            top = jnp.maximum(lse_old, lse)
            w_old, w_new = jnp.exp(lse_old - top), jnp.exp(lse - top)
            tot = w_old + w_new
            o = (o_in_ref[:, hs].astype(F32) * w_old + o * w_new) / tot
            lse = top + jnp.log(tot)
        o_ref[:, hs] = o.astype(o_ref.dtype)
        return jnp.where(lane == h, lse, lse_tile)

    lse_tile = lax.fori_loop(0, n_heads, head, jnp.zeros((blk, LANES), F32))
    if not last:
        lse_ref[...] = lse_tile


def _attn_branch(q, kv, state, *, batch, seq, dilation, last):
    m, width = q.shape
    n_heads = width // HEAD_DIM
    assert n_heads <= LANES
    L = seq // dilation
    assert L % ATTN_BLOCK == 0
    nb = L // ATTN_BLOCK
    blk = ATTN_BLOCK
    has_prev_block, has_state = nb > 1, state is not None

    def view(a, w):
        return a.reshape(batch, L, dilation * w)

    q_spec = pl.BlockSpec((None, blk, width), lambda b, r, j: (b, j, r))
    lse_spec = pl.BlockSpec((None, blk, LANES), lambda b, r, j: (b, j, r))
    k_cur = pl.BlockSpec((None, blk, width), lambda b, r, j: (b, j, 2 * r))
    v_cur = pl.BlockSpec((None, blk, width), lambda b, r, j: (b, j, 2 * r + 1))
    inputs, in_specs = [view(q, width), view(kv, 2 * width), view(kv, 2 * width)], [q_spec, k_cur, v_cur]
    if has_prev_block:
        inputs += [view(kv, 2 * width)] * 2
        in_specs += [pl.BlockSpec((None, blk, width), lambda b, r, j: (b, jnp.maximum(j - 1, 0), 2 * r)),
                     pl.BlockSpec((None, blk, width), lambda b, r, j: (b, jnp.maximum(j - 1, 0), 2 * r + 1))]
    if has_state:
        inputs += [view(state[0], width), view(state[1], LANES)]
        in_specs += [q_spec, lse_spec]
    out_shape = [jax.ShapeDtypeStruct((batch, L, dilation * width), BF16)]
    out_specs = [q_spec]
    if not last:
        out_shape.append(jax.ShapeDtypeStruct((batch, L, dilation * LANES), F32))
        out_specs.append(lse_spec)
    res = pl.pallas_call(
        functools.partial(_attn_body, n_heads=n_heads, has_prev_block=has_prev_block,
                          has_state=has_state, last=last),
        grid=(batch, dilation, nb),
        in_specs=in_specs, out_specs=out_specs, out_shape=out_shape,
        compiler_params=_params(("arbitrary",) * 3, 32),
        name=f"dilated_attn_d{dilation}",
    )(*inputs)
    o = res[0].reshape(m, width)
    return (o, None) if last else (o, res[1].reshape(m, LANES))


def _router_body(h_ref, gain_ref, r_ref, idx_ref, gate_ref):
    x = h_ref[...]
    hn = x * _rms_scale(x) * gain_ref[...]
    logits = jnp.dot(hn, r_ref[...], preferred_element_type=F32, precision=lax.Precision.HIGHEST)
    lane = lax.broadcasted_iota(jnp.int32, logits.shape, 1).astype(F32)
    logits = jnp.where(lane < N_EXPERTS, logits, NEG)
    m1 = jnp.max(logits, axis=-1, keepdims=True)
    i1 = jnp.min(jnp.where(logits == m1, lane, float(LANES)), axis=-1, keepdims=True)
    rest = jnp.where(lane == i1, NEG, logits)
    m2 = jnp.max(rest, axis=-1, keepdims=True)
    i2 = jnp.min(jnp.where(rest == m2, lane, float(LANES)), axis=-1, keepdims=True)
    e2 = jnp.exp(m2 - m1)
    g1 = 1.0 / (1.0 + e2)
    g2 = e2 / (1.0 + e2)
    idx_ref[...] = jnp.where(lane == 0, i1, jnp.where(lane == 1, i2, 0.0)).astype(jnp.int32)
    gate_ref[...] = jnp.where(lane == 0, g1, jnp.where(lane == 1, g2, 0.0))


def _router(h, gain, router):
    m, d = h.shape
    n_e = router.shape[1]
    assert n_e == N_EXPERTS
    tm = _tile(m, 256, 8)
    r_pad = jnp.zeros((d, LANES), F32).at[:, :n_e].set(router)
    row = pl.BlockSpec((tm, d), lambda i: (i, 0))
    small = pl.BlockSpec((tm, LANES), lambda i: (i, 0))
    return pl.pallas_call(
        _router_body,
        grid=(m // tm,),
        in_specs=[row, pl.BlockSpec((1, d), lambda i: (0, 0)), pl.BlockSpec((d, LANES), lambda i: (0, 0))],
        out_specs=[small, small],
        out_shape=[jax.ShapeDtypeStruct((m, LANES), jnp.int32), jax.ShapeDtypeStruct((m, LANES), F32)],
        compiler_params=_params(("arbitrary",), 32),
        name="router_top2",
    )(h, gain.reshape(1, d), r_pad)


def _dispatch_plan(top_idx, tm):
    m = top_idx.shape[0]
    n_tiles = (m * TOP_K + N_EXPERTS * (tm - 1)) // tm
    flat_e = top_idx.reshape(-1)
    onehot = (flat_e[:, None] == jnp.arange(N_EXPERTS)[None, :]).astype(jnp.int32)
    rank = jnp.sum((jnp.cumsum(onehot, axis=0) - 1) * onehot, axis=1)
    counts = jnp.sum(onehot, axis=0)
    tiles_per = (counts + tm - 1) // tm
    tile_end = jnp.cumsum(tiles_per)
    row_start = (tile_end - tiles_per) * tm
    dest = row_start[flat_e] + rank
    src = jnp.zeros((n_tiles * tm,), jnp.int32).at[dest].set(jnp.arange(m * TOP_K, dtype=jnp.int32) // TOP_K)
    n_used = tile_end[-1]
    t = jnp.minimum(jnp.arange(n_tiles, dtype=jnp.int32), n_used - 1)
    tile_expert = jnp.sum((t[:, None] >= tile_end[None, :]).astype(jnp.int32), axis=1)
    tile_first = (t == (tile_end - tiles_per)[tile_expert]).astype(jnp.int32)
    tile_valid = (jnp.arange(n_tiles) < n_used).astype(jnp.int32)
    return (src, dest.reshape(m, TOP_K).astype(jnp.int32), tile_expert.astype(jnp.int32), t,
            tile_first, tile_valid, (n_used * tm).astype(jnp.int32).reshape(1))


def _gather_body(src_ref, nrows_ref, gain_ref, h_hbm, o_ref, buf_ref, sem, *, rows):
    base = pl.program_id(0) * rows
    used = base < nrows_ref[0]

    def row_copy(j):
        return pltpu.make_async_copy(h_hbm.at[pl.ds(src_ref[base + j], 1)], buf_ref.at[pl.ds(j, 1)], sem)

    @pl.when(jnp.logical_not(used))
    def _():
        o_ref[...] = jnp.zeros(o_ref.shape, o_ref.dtype)

    @pl.when(used)
    def _():
        def start(j, c):
            row_copy(j).start()
            return c

        def wait(j, c):
            row_copy(j).wait()
            return c

        lax.fori_loop(0, rows, start, 0)
        lax.fori_loop(0, rows, wait, 0)
        x = buf_ref[...]
        o_ref[...] = (x * _rms_scale(x) * gain_ref[...]).astype(BF16)


def _gather_norm_rows(h, gain, src, n_rows_used, rows):
    p = src.shape[0]
    d = h.shape[1]
    return pl.pallas_call(
        functools.partial(_gather_body, rows=rows),
        grid_spec=pltpu.PrefetchScalarGridSpec(
            num_scalar_prefetch=2,
            grid=(p // rows,),
            in_specs=[pl.BlockSpec((1, d), lambda i, src, nr: (0, 0)), pl.BlockSpec(memory_space=pl.ANY)],
            out_specs=pl.BlockSpec((rows, d), lambda i, src, nr: (i, 0)),
            scratch_shapes=[pltpu.VMEM((rows, d), F32), pltpu.SemaphoreType.DMA],
        ),
        out_shape=jax.ShapeDtypeStruct((p, d), BF16),
        compiler_params=_params(("arbitrary",), 32),
        name="moe_gather_rows",
    )(src, n_rows_used, gain.reshape(1, d), h)


def _expert_changed(pf, n, m):
    return pf[2][m] == 1


def _tile_valid(pf, n, m):
    return pf[3][m] == 1


def _moe_gate_up(xs, w_gate_up, plan, tm):
    p, k = xs.shape
    dff = w_gate_up.shape[2] // 2
    tn = _tile(dff, 256, LANES)
    nt = dff // tn

    def epilogue(acc, e_refs, o_refs, scratch, pf, n, mi):
        o_refs[0][...] = _silu_mul(acc, tn).astype(BF16)

    return _ws_matmul(
        xs, [w_gate_up, w_gate_up], [],
        grid=(nt, p // tm),
        x_spec=pl.BlockSpec((tm, k), lambda n, mi, te, tr, tf, tv: (tr[mi], 0)),
        w_specs=[pl.BlockSpec((None, k, tn), lambda n, mi, te, tr, tf, tv: (te[mi], 0, n)),
                 pl.BlockSpec((None, k, tn), lambda n, mi, te, tr, tf, tv: (te[mi], 0, n + nt))],
        extra_specs=[],
        out_shape=jax.ShapeDtypeStruct((p, dff), BF16),
        out_specs=pl.BlockSpec((tm, tn), lambda n, mi, te, tr, tf, tv: (mi, n)),
        epilogue=epilogue, tn=tn, cast=True, prefetch=plan, recast=_expert_changed, valid=_tile_valid,
        name="moe_gate_up")


def _moe_down(hs, w_down, plan, tm):
    p, k = hs.shape
    d = w_down.shape[2]
    tn = _tile(d, 512, LANES)

    def epilogue(acc, e_refs, o_refs, scratch, pf, n, mi):
        o_refs[0][...] = acc

    return _ws_matmul(
        hs, [w_down], [],
        grid=(d // tn, p // tm),
        x_spec=pl.BlockSpec((tm, k), lambda n, mi, te, tr, tf, tv: (tr[mi], 0)),
        w_specs=[pl.BlockSpec((None, k, tn), lambda n, mi, te, tr, tf, tv: (te[mi], 0, n))],
        extra_specs=[],
        out_shape=jax.ShapeDtypeStruct((p, d), F32),
        out_specs=pl.BlockSpec((tm, tn), lambda n, mi, te, tr, tf, tv: (mi, n)),
        epilogue=epilogue, tn=tn, cast=True, prefetch=plan, recast=_expert_changed, valid=_tile_valid,
        name="moe_down")


def _combine_body(pos_ref, h_ref, gate_ref, gain_ref, ys_hbm, h_out_ref, hn_ref, buf_ref, sem, *, tm):
    base = pl.program_id(0) * tm

    def row_copy(j, s):
        return pltpu.make_async_copy(ys_hbm.at[pl.ds(pos_ref[(base + j) * TOP_K + s], 1)],
                                     buf_ref.at[s, pl.ds(j, 1)], sem)

    def start(j, c):
        for s in range(TOP_K):
            row_copy(j, s).start()
        return c

    def wait(j, c):
        for s in range(TOP_K):
            row_copy(j, s).wait()
        return c

    lax.fori_loop(0, tm, start, 0)
    lax.fori_loop(0, tm, wait, 0)
    gates = gate_ref[...]
    h = h_ref[...]
    for s in range(TOP_K):
        h = h + gates[:, s:s + 1] * buf_ref[s]
    h_out_ref[...] = h
    hn_ref[...] = (h * _rms_scale(h) * gain_ref[...]).astype(BF16)


def _moe_combine(h, ys, pos, gates, next_gain):
    m, d = h.shape
    tm = _tile(m, 256, 8)
    row = pl.BlockSpec((tm, d), lambda i, pos: (i, 0))
    return pl.pallas_call(
        functools.partial(_combine_body, tm=tm),
        grid_spec=pltpu.PrefetchScalarGridSpec(
            num_scalar_prefetch=1,
            grid=(m // tm,),
            in_specs=[row, pl.BlockSpec((tm, LANES), lambda i, pos: (i, 0)),
                      pl.BlockSpec((1, d), lambda i, pos: (0, 0)), pl.BlockSpec(memory_space=pl.ANY)],
            out_specs=[row, row],
            scratch_shapes=[pltpu.VMEM((TOP_K, tm, d), F32), pltpu.SemaphoreType.DMA],
        ),
        out_shape=[jax.ShapeDtypeStruct((m, d), F32), jax.ShapeDtypeStruct((m, d), BF16)],
        compiler_params=_params(("arbitrary",), 48),
        name="moe_combine",
    )(pos.reshape(-1), h, gates, next_gain.reshape(1, d), ys)


def kernel(x, p, pool_norm, pool_w, pool_scale, kv_norm, w_kv, k_norm, attn_norm, w_q, q_norm, w_o,
           ffn_norm, dense_w_gate_up, dense_w_down, moe_router, moe_w_gate_up, moe_w_down, ple_w,
           ple_norm, ple_gate_w):
    batch, seq, d = x.shape
    m = batch * seq
    p_bf16 = p.astype(BF16).reshape(p.shape[0], m, p.shape[-1])
    cos, sin_signed = _rope_tables(seq)

    h, hn = _pool_layer(x.reshape(m, d), seq, pool_norm[0], pool_w[0], pool_scale[0], ffn_norm[0])
    act = _gate_up(hn, dense_w_gate_up[0])
    h = _matmul_residual(act, dense_w_down[0].astype(BF16), h, name="dense_down")
    (hn,) = _rms_cast(h, [ple_norm[0]])
    h = _ple(hn, ple_gate_w[0], p_bf16[0], ple_w[0], h)

    hn_kv, hn_q = _rms_cast(h, [kv_norm, attn_norm[0]])
    width = w_q.shape[2]
    kv = _qk_projection(hn_kv, w_kv, k_norm, cos, sin_signed, seq, n_rope_cols=width,
                        post_scale=1.0, name="kv_proj")
    q = _qk_projection(hn_q, w_q[0], q_norm[0], cos, sin_signed, seq, n_rope_cols=width,
                       post_scale=1.0 / math.sqrt(HEAD_DIM), name="q_proj")

    state = None
    for bi, (window, dilation) in enumerate(DILATED_BRANCHES):
        assert window // dilation == ATTN_BLOCK
        state = _attn_branch(q, kv, state, batch=batch, seq=seq, dilation=dilation,
                             last=bi == len(DILATED_BRANCHES) - 1)
    h = _matmul_residual(state[0], w_o[0], h, name="attn_out_proj")

    top_idx, gates = _router(h, ffn_norm[1], moe_router[0])
    tm_moe = _tile(m, 512, 8)
    src, pos, tile_expert, tile_row, tile_first, tile_valid, n_rows_used = _dispatch_plan(
        top_idx[:, :TOP_K], tm_moe)
    plan = (tile_expert, tile_row, tile_first, tile_valid)
    xs = _gather_norm_rows(h, ffn_norm[1], src, n_rows_used, tm_moe)
    hs = _moe_gate_up(xs, moe_w_gate_up[0], plan, tm_moe)
    ys = _moe_down(hs, moe_w_down[0], plan, tm_moe)
    h, hn = _moe_combine(h, ys, pos, gates, ple_norm[1])
    h = _ple(hn, ple_gate_w[1], p_bf16[1], ple_w[1], h)
    return h.reshape(batch, seq, d)
```

```python
import functools
import math

import jax
import jax.numpy as jnp
from jax import lax
from jax.experimental import pallas as pl
from jax.experimental.pallas import tpu as pltpu

F32 = jnp.float32
BF16 = jnp.bfloat16

HEAD_DIM = 128
LANES = 128
ROPE_THETA = 10000.0
RMS_EPS = 1e-6
POOL_WINDOWS = (2, 4, 8, 16)
POOL_HALO = 16
DILATED_BRANCHES = ((128, 1), (512, 4), (2048, 16))
ATTN_BLOCK = 128
N_EXPERTS = 8
TOP_K = 2
NEG = -1e30
MIB = 1024 * 1024


def _params(semantics, vmem_mib):
    return pltpu.CompilerParams(dimension_semantics=semantics, vmem_limit_bytes=vmem_mib * MIB)


def _tile(dim, pref, quantum):
    if dim <= pref:
        return dim
    t = (pref // quantum) * quantum
    while t > quantum and dim % t:
        t -= quantum
    assert dim % t == 0, (dim, pref, quantum)
    return t


def _rms_scale(x):
    return lax.rsqrt(jnp.mean(x * x, axis=-1, keepdims=True) + RMS_EPS)


def _pool_body(x_ref, halo_ref, xcol_ref, w_ref, gain_ref, scale_ref, next_gain_ref,
               h_ref, hn_ref, yn_ref, hrow_ref, *, ts, seq, n_groups, gdim):
    i = pl.program_id(0)
    g = pl.program_id(1)
    t0 = (i * ts) % seq

    @pl.when(g == 0)
    def _():
        x = x_ref[...]
        y = x * _rms_scale(x) * gain_ref[...]
        xh = halo_ref[...]
        yh = xh * _rms_scale(xh) * gain_ref[...]
        yh = yh * (t0 > 0).astype(F32)
        for gi in range(n_groups):
            yn_ref[gi, :POOL_HALO, :] = yh[:, gi * gdim:(gi + 1) * gdim]
            yn_ref[gi, POOL_HALO:, :] = y[:, gi * gdim:(gi + 1) * gdim]

    pos = t0 + lax.broadcasted_iota(jnp.int32, (ts, 1), 0)
    for gi, win in enumerate(POOL_WINDOWS):
        @pl.when(g == gi)
        def _():
            cur = yn_ref[gi, POOL_HALO:, :]
            acc = cur
            for j in range(1, win):
                acc = acc + yn_ref[gi, POOL_HALO - j:POOL_HALO - j + ts, :]
            inv_cnt = 1.0 / jnp.minimum(pos + 1, win).astype(F32)
            pooled = (acc * inv_cnt - cur).astype(BF16)
            mixed = jnp.dot(pooled, w_ref[...].astype(BF16), preferred_element_type=F32)
            hrow_ref[gi] = xcol_ref[...] + mixed * scale_ref[...]

    h_ref[...] = hrow_ref[g]

    @pl.when(g == n_groups - 1)
    def _():
        ss = jnp.zeros((ts, 1), F32)
        for gi in range(n_groups):
            hg = hrow_ref[gi]
            ss = ss + jnp.sum(hg * hg, axis=-1, keepdims=True)
        inv = lax.rsqrt(ss / (n_groups * gdim) + RMS_EPS)
        for gi in range(n_groups):
            sl = slice(gi * gdim, (gi + 1) * gdim)
            hn_ref[:, sl] = (hrow_ref[gi] * inv * next_gain_ref[:, sl]).astype(BF16)


def _pool_layer(x2, seq, norm_gain, w_groups, scale, next_gain):
    m, d = x2.shape
    n_groups, gdim = w_groups.shape[0], w_groups.shape[1]
    assert n_groups == len(POOL_WINDOWS) and n_groups * gdim == d
    ts = _tile(seq, 256, POOL_HALO)
    hpt = ts // POOL_HALO
    body = functools.partial(_pool_body, ts=ts, seq=seq, n_groups=n_groups, gdim=gdim)
    return pl.pallas_call(
        body,
        grid=(m // ts, n_groups),
        in_specs=[
            pl.BlockSpec((ts, d), lambda i, g: (i, 0)),
            pl.BlockSpec((POOL_HALO, d), lambda i, g: (jnp.maximum(i * hpt - 1, 0), 0)),
            pl.BlockSpec((ts, gdim), lambda i, g: (i, g)),
            pl.BlockSpec((None, gdim, gdim), lambda i, g: (g, 0, 0)),
            pl.BlockSpec((1, d), lambda i, g: (0, 0)),
            pl.BlockSpec((1, gdim), lambda i, g: (0, g)),
            pl.BlockSpec((1, d), lambda i, g: (0, 0)),
        ],
        out_specs=[
            pl.BlockSpec((ts, gdim), lambda i, g: (i, g)),
            pl.BlockSpec((ts, d), lambda i, g: (i, 0)),
        ],
        out_shape=[jax.ShapeDtypeStruct((m, d), F32), jax.ShapeDtypeStruct((m, d), BF16)],
        scratch_shapes=[
            pltpu.VMEM((n_groups, ts + POOL_HALO, gdim), F32),
            pltpu.VMEM((n_groups, ts, gdim), F32),
        ],
        compiler_params=_params(("arbitrary", "arbitrary"), 48),
        name="pool_layer",
    )(x2, x2, x2, w_groups, norm_gain.reshape(1, d), scale.reshape(1, d), next_gain.reshape(1, d))


def _rms_body(x_ref, *refs, n_out):
    gains, outs = refs[:n_out], refs[n_out:]
    x = x_ref[...]
    y = x * _rms_scale(x)
    for g_ref, o_ref in zip(gains, outs):
        o_ref[...] = (y * g_ref[...]).astype(BF16)


def _rms_cast(x2, gains):
    m, d = x2.shape
    tm = _tile(m, 256, 8)
    n_out = len(gains)
    row = pl.BlockSpec((tm, d), lambda i: (i, 0))
    vec = pl.BlockSpec((1, d), lambda i: (0, 0))
    return pl.pallas_call(
        functools.partial(_rms_body, n_out=n_out),
        grid=(m // tm,),
        in_specs=[row] + [vec] * n_out,
        out_specs=[row] * n_out,
        out_shape=[jax.ShapeDtypeStruct((m, d), BF16)] * n_out,
        compiler_params=_params(("arbitrary",), 32),
        name="rms_cast",
    )(x2, *[g.reshape(1, d) for g in gains])


def _ws_matmul(x, ws, extras, *, grid, x_spec, w_specs, extra_specs, out_shape, out_specs,
               epilogue, tn, recast, k_block=None, prefetch=(), valid=None, vmem_mib=48,
               name="ws_matmul"):
    n_w, n_e, n_pf = len(ws), len(extras), len(prefetch)
    outs = out_shape if isinstance(out_shape, (list, tuple)) else [out_shape]
    n_out = len(outs)
    k = x.shape[1] if k_block is None else k_block
    cast_rows = _tile(k, 512, 16)

    def body(*refs):
        pf, refs = refs[:n_pf], refs[n_pf:]
        x_ref = refs[0]
        w_refs = refs[1:1 + n_w]
        e_refs = refs[1 + n_w:1 + n_w + n_e]
        o_refs = refs[1 + n_w + n_e:1 + n_w + n_e + n_out]
        wb_ref = refs[1 + n_w + n_e + n_out]
        n, m = pl.program_id(0), pl.program_id(1)

        def compute():
            @pl.when(recast(pf, n, m))
            def _():
                def chunk(c, carry):
                    r0 = pl.multiple_of(c * cast_rows, cast_rows)
                    for j, w_ref in enumerate(w_refs):
                        wb_ref[pl.ds(r0, cast_rows), j * tn:(j + 1) * tn] = (
                            w_ref[pl.ds(r0, cast_rows), :].astype(BF16))
                    return carry
                lax.fori_loop(0, k // cast_rows, chunk, 0)

            acc = jnp.dot(x_ref[...], wb_ref[...], preferred_element_type=F32)
            epilogue(acc, e_refs, o_refs, pf, n, m)

        if valid is None:
            compute()
        else:
            ok = valid(pf, n, m)
            pl.when(ok)(compute)

            @pl.when(jnp.logical_not(ok))
            def _():
                for o_ref in o_refs:
                    o_ref[...] = jnp.zeros(o_ref.shape, o_ref.dtype)

    return pl.pallas_call(
        body,
        grid_spec=pltpu.PrefetchScalarGridSpec(
            num_scalar_prefetch=n_pf,
            grid=grid,
            in_specs=[x_spec] + list(w_specs) + list(extra_specs),
            out_specs=out_specs,
            scratch_shapes=[pltpu.VMEM((k, n_w * tn), BF16)],
        ),
        out_shape=out_shape,
        compiler_params=_params(("arbitrary", "arbitrary"), vmem_mib),
        name=name,
    )(*prefetch, x, *ws, *extras)


def _first_row_tile(pf, n, m):
    return m == 0


def _silu_mul(acc, tn):
    gate, up = acc[:, :tn], acc[:, tn:]
    return gate * jax.nn.sigmoid(gate) * up


def _gate_up(xn, w_gate_up):
    m, k = xn.shape
    dff = w_gate_up.shape[1] // 2
    tm, tn = _tile(m, 1024, 8), _tile(dff, 256, LANES)
    nt = dff // tn

    def epilogue(acc, e_refs, o_refs, pf, n, mi):
        o_refs[0][...] = _silu_mul(acc, tn).astype(BF16)

    return _ws_matmul(
        xn, [w_gate_up, w_gate_up], [],
        grid=(nt, m // tm),
        x_spec=pl.BlockSpec((tm, k), lambda n, mi: (mi, 0)),
        w_specs=[pl.BlockSpec((k, tn), lambda n, mi: (0, n)),
                 pl.BlockSpec((k, tn), lambda n, mi: (0, n + nt))],
        extra_specs=[],
        out_shape=jax.ShapeDtypeStruct((m, dff), BF16),
        out_specs=pl.BlockSpec((tm, tn), lambda n, mi: (mi, n)),
        epilogue=epilogue, tn=tn, recast=_first_row_tile, name="dense_gate_up")


MAX_WEIGHT_TILE_ROWS = 6144


def _matmul_residual(x, w, res, *, name):
    m, k = x.shape
    n_out = w.shape[1]
    tm, tn = _tile(m, 512, 8), _tile(n_out, 512, LANES)
    n_chunks = pl.cdiv(k, MAX_WEIGHT_TILE_ROWS)
    assert k % (n_chunks * LANES) == 0
    kc = k // n_chunks

    def epilogue(acc, e_refs, o_refs, pf, n, mi):
        o_refs[0][...] = e_refs[0][...] + acc

    for c in range(n_chunks):
        res = _ws_matmul(
            x, [w], [res],
            grid=(n_out // tn, m // tm),
            x_spec=pl.BlockSpec((tm, kc), lambda n, mi, c=c: (mi, c)),
            w_specs=[pl.BlockSpec((kc, tn), lambda n, mi, c=c: (c, n))],
            extra_specs=[pl.BlockSpec((tm, tn), lambda n, mi: (mi, n))],
            out_shape=jax.ShapeDtypeStruct((m, n_out), F32),
            out_specs=pl.BlockSpec((tm, tn), lambda n, mi: (mi, n)),
            epilogue=epilogue, tn=tn, k_block=kc, recast=_first_row_tile, vmem_mib=56,
            name=name)
    return res


def _ple(hn, w_gate, p_bf16, w_ple, h):
    m, k = hn.shape
    d = w_gate.shape[1]
    pdim = p_bf16.shape[1]
    tm, tn = _tile(m, 1024, 8), _tile(d, 512, LANES)

    def epilogue(acc, e_refs, o_refs, pf, n, mi):
        p_ref, wp_ref, h_ref = e_refs
        emb = jnp.dot(p_ref[...], wp_ref[...].astype(BF16), preferred_element_type=F32)
        o_refs[0][...] = h_ref[...] + emb * jax.nn.sigmoid(acc)

    return _ws_matmul(
        hn, [w_gate], [p_bf16, w_ple, h],
        grid=(d // tn, m // tm),
        x_spec=pl.BlockSpec((tm, k), lambda n, mi: (mi, 0)),
        w_specs=[pl.BlockSpec((k, tn), lambda n, mi: (0, n))],
        extra_specs=[pl.BlockSpec((tm, pdim), lambda n, mi: (mi, 0)),
                     pl.BlockSpec((pdim, tn), lambda n, mi: (0, n)),
                     pl.BlockSpec((tm, tn), lambda n, mi: (mi, n))],
        out_shape=jax.ShapeDtypeStruct((m, d), F32),
        out_specs=pl.BlockSpec((tm, tn), lambda n, mi: (mi, n)),
        epilogue=epilogue, tn=tn, recast=_first_row_tile, vmem_mib=56, name="ple")


def _rope_tables(seq):
    half = HEAD_DIM // 2
    inv_freq = jnp.exp(-math.log(ROPE_THETA) * jnp.arange(half, dtype=F32) / half)
    ang = jnp.arange(seq, dtype=F32)[:, None] * inv_freq[None, :]
    cos, sin = jnp.cos(ang), jnp.sin(ang)
    return jnp.concatenate([cos, cos], axis=-1), jnp.concatenate([-sin, sin], axis=-1)


def _norm_rope_heads(acc, gain, cos, sin_signed, out_ref, tn, post_scale):
    for hh in range(tn // HEAD_DIM):
        sl = slice(hh * HEAD_DIM, (hh + 1) * HEAD_DIM)
        y = acc[:, sl]
        y = y * _rms_scale(y) * gain
        y = y * cos + pltpu.roll(y, HEAD_DIM // 2, axis=1) * sin_signed
        if post_scale != 1.0:
            y = y * post_scale
        out_ref[:, sl] = y


def _qk_projection(xn, w, head_gain, cos, sin_signed, seq, *, n_rope_cols, post_scale, name):
    m, k = xn.shape
    n_out = w.shape[1]
    tm, tn = _tile(seq, 1024, 8), _tile(min(n_out, n_rope_cols), 512, LANES)
    n_rope_tiles = n_rope_cols // tn
    spb = seq // tm

    def epilogue(acc, e_refs, o_refs, pf, n, mi):
        g_ref, c_ref, s_ref = e_refs

        @pl.when(n < n_rope_tiles)
        def _():
            _norm_rope_heads(acc, g_ref[...], c_ref[...], s_ref[...], o_refs[0], tn, post_scale)

        if n_rope_tiles * tn < n_out:
            @pl.when(n >= n_rope_tiles)
            def _():
                o_refs[0][...] = acc

    return _ws_matmul(
        xn, [w], [head_gain.reshape(1, HEAD_DIM), cos, sin_signed],
        grid=(n_out // tn, m // tm),
        x_spec=pl.BlockSpec((tm, k), lambda n, mi: (mi, 0)),
        w_specs=[pl.BlockSpec((k, tn), lambda n, mi: (0, n))],
        extra_specs=[pl.BlockSpec((1, HEAD_DIM), lambda n, mi: (0, 0)),
                     pl.BlockSpec((tm, HEAD_DIM), lambda n, mi: (mi % spb, 0)),
                     pl.BlockSpec((tm, HEAD_DIM), lambda n, mi: (mi % spb, 0))],
        out_shape=jax.ShapeDtypeStruct((m, n_out), F32),
        out_specs=pl.BlockSpec((tm, tn), lambda n, mi: (mi, n)),
        epilogue=epilogue, tn=tn, recast=_first_row_tile, vmem_mib=56, name=name)


def _attn_body(q_ref, k_ref, v_ref, o_ref, acc_ref, m_ref, l_ref, *, seq):
    blk = ATTN_BLOCK
    row = lax.broadcasted_iota(jnp.int32, (blk, blk), 0)
    col = lax.broadcasted_iota(jnp.int32, (blk, blk), 1)
    bias_cur = jnp.where(col <= row, 0.0, NEG).astype(F32)
    bias_prev = jnp.where(col >= row, 0.0, NEG).astype(F32)
    contract_last = (((1,), (1,)), ((), ()))
    order = sorted(DILATED_BRANCHES, key=lambda wd: -wd[1])
    assert order[-1][1] == 1

    def rows_of(start, d):
        return pl.ds(start, blk) if d == 1 else pl.ds(start, blk, stride=d)

    def tile(d, start, prev_start, first, last):
        cur = rows_of(start, d)
        q = q_ref[cur, :].astype(BF16)
        s_c = lax.dot_general(q, k_ref[cur, :].astype(BF16), contract_last,
                              preferred_element_type=F32) + bias_cur
        mx = jnp.max(s_c, axis=-1, keepdims=True)
        if prev_start is not None:
            prev = rows_of(prev_start, d)
            s_p = lax.dot_general(q, k_ref[prev, :].astype(BF16), contract_last,
                                  preferred_element_type=F32) + bias_prev
            mx = jnp.maximum(mx, jnp.max(s_p, axis=-1, keepdims=True))
        if first:
            m_new = jnp.broadcast_to(mx, (blk, LANES))
        else:
            m_old = m_ref[cur, :]
            m_new = jnp.maximum(m_old, mx)
        p_c = jnp.exp(s_c - m_new)
        den = jnp.sum(p_c, axis=-1, keepdims=True)
        acc = jnp.dot(p_c.astype(BF16), v_ref[cur, :].astype(BF16), preferred_element_type=F32)
        if prev_start is not None:
            p_p = jnp.exp(s_p - m_new)
            den = den + jnp.sum(p_p, axis=-1, keepdims=True)
            acc = acc + jnp.dot(p_p.astype(BF16), v_ref[prev, :].astype(BF16),
                                preferred_element_type=F32)
        if first:
            den = jnp.broadcast_to(den, (blk, LANES))
        else:
            alpha = jnp.exp(m_old - m_new)
            den = alpha * l_ref[cur, :] + den
            acc = alpha * acc_ref[cur, :] + acc
        if last:
            o_ref[cur, :] = (acc / den).astype(o_ref.dtype)
        else:
            acc_ref[cur, :] = acc
            m_ref[cur, :] = m_new
            l_ref[cur, :] = den

    for step, (window, d) in enumerate(order):
        nb = seq // d // blk
        for r in range(d):
            for i in range(nb):
                start = r + d * blk * i
                tile(d, start, start - d * blk if i > 0 else None, step == 0, step == len(order) - 1)


def _dilated_attention(q, kv, *, batch, seq):
    m, width = q.shape
    assert ATTN_BLOCK == LANES
    for window, d in DILATED_BRANCHES:
        assert window // d == ATTN_BLOCK and seq % (d * ATTN_BLOCK) == 0
    n_heads = width // HEAD_DIM
    head_block = pl.BlockSpec((seq, HEAD_DIM), lambda b, h: (b, h))
    return pl.pallas_call(
        functools.partial(_attn_body, seq=seq),
        grid=(batch, n_heads),
        in_specs=[head_block, head_block, pl.BlockSpec((seq, HEAD_DIM), lambda b, h: (b, n_heads + h))],
        out_specs=head_block,
        out_shape=jax.ShapeDtypeStruct((m, width), BF16),
        scratch_shapes=[pltpu.VMEM((seq, LANES), F32)] * 3,
        compiler_params=_params(("arbitrary", "arbitrary"), 48),
        name="dilated_attention",
    )(q, kv, kv)


def _router_body(h_ref, gain_ref, r_ref, idx_ref, gate_ref):
    x = h_ref[...]
    hn = x * _rms_scale(x) * gain_ref[...]
    logits = jnp.dot(hn, r_ref[...], preferred_element_type=F32, precision=lax.Precision.HIGHEST)
    lane = lax.broadcasted_iota(jnp.int32, logits.shape, 1).astype(F32)
    logits = jnp.where(lane < N_EXPERTS, logits, NEG)
    m1 = jnp.max(logits, axis=-1, keepdims=True)
    i1 = jnp.min(jnp.where(logits == m1, lane, float(LANES)), axis=-1, keepdims=True)
    rest = jnp.where(lane == i1, NEG, logits)
    m2 = jnp.max(rest, axis=-1, keepdims=True)
    i2 = jnp.min(jnp.where(rest == m2, lane, float(LANES)), axis=-1, keepdims=True)
    e2 = jnp.exp(m2 - m1)
    g1 = 1.0 / (1.0 + e2)
    g2 = e2 / (1.0 + e2)
    idx_ref[...] = jnp.where(lane == 0, i1, jnp.where(lane == 1, i2, 0.0)).astype(jnp.int32)
    gate_ref[...] = jnp.where(lane == 0, g1, jnp.where(lane == 1, g2, 0.0))


def _router(h, gain, router):
    m, d = h.shape
    n_e = router.shape[1]
    assert n_e == N_EXPERTS
    tm = _tile(m, 256, 8)
    r_pad = jnp.zeros((d, LANES), F32).at[:, :n_e].set(router)
    row = pl.BlockSpec((tm, d), lambda i: (i, 0))
    small = pl.BlockSpec((tm, LANES), lambda i: (i, 0))
    return pl.pallas_call(
        _router_body,
        grid=(m // tm,),
        in_specs=[row, pl.BlockSpec((1, d), lambda i: (0, 0)), pl.BlockSpec((d, LANES), lambda i: (0, 0))],
        out_specs=[small, small],
        out_shape=[jax.ShapeDtypeStruct((m, LANES), jnp.int32), jax.ShapeDtypeStruct((m, LANES), F32)],
        compiler_params=_params(("arbitrary",), 32),
        name="router_top2",
    )(h, gain.reshape(1, d), r_pad)


def _dispatch_plan(top_idx, tm):
    m = top_idx.shape[0]
    n_tiles = (m * TOP_K + N_EXPERTS * (tm - 1)) // tm
    flat_e = top_idx.reshape(-1)
    onehot = (flat_e[:, None] == jnp.arange(N_EXPERTS)[None, :]).astype(jnp.int32)
    rank = jnp.sum((jnp.cumsum(onehot, axis=0) - 1) * onehot, axis=1)
    counts = jnp.sum(onehot, axis=0)
    tiles_per = (counts + tm - 1) // tm
    tile_end = jnp.cumsum(tiles_per)
    row_start = (tile_end - tiles_per) * tm
    dest = row_start[flat_e] + rank
    src = jnp.zeros((n_tiles * tm,), jnp.int32).at[dest].set(jnp.arange(m * TOP_K, dtype=jnp.int32) // TOP_K)
    n_used = tile_end[-1]
    t = jnp.minimum(jnp.arange(n_tiles, dtype=jnp.int32), n_used - 1)
    tile_expert = jnp.sum((t[:, None] >= tile_end[None, :]).astype(jnp.int32), axis=1)
    tile_first = (t == (tile_end - tiles_per)[tile_expert]).astype(jnp.int32)
    tile_valid = (jnp.arange(n_tiles) < n_used).astype(jnp.int32)
    return (src, dest.reshape(m, TOP_K).astype(jnp.int32), tile_expert.astype(jnp.int32), t,
            tile_first, tile_valid, (n_used * tm).astype(jnp.int32).reshape(1))


DMA_LOOP_UNROLL = 8


def _row_dma_loop(n, copy_of, action):
    def step(j, c):
        getattr(copy_of(j), action)()
        return c
    lax.fori_loop(0, n, step, 0, unroll=DMA_LOOP_UNROLL)


def _gather_body(src_ref, nrows_ref, gain_ref, h_hbm, o_ref, buf_ref, sems, *, rows):
    i = pl.program_id(0)
    slot = i % 2

    def tile_used(t):
        return t * rows < nrows_ref[0]

    def row_copy(t, s):
        return lambda j: pltpu.make_async_copy(
            h_hbm.at[pl.ds(src_ref[t * rows + j], 1)], buf_ref.at[s, pl.ds(j, 1)], sems.at[s])

    @pl.when((i == 0) & tile_used(0))
    def _():
        _row_dma_loop(rows, row_copy(0, 0), "start")

    @pl.when((i + 1 < pl.num_programs(0)) & tile_used(i + 1))
    def _():
        _row_dma_loop(rows, row_copy(i + 1, 1 - slot), "start")

    @pl.when(jnp.logical_not(tile_used(i)))
    def _():
        o_ref[...] = jnp.zeros(o_ref.shape, o_ref.dtype)

    @pl.when(tile_used(i))
    def _():
        _row_dma_loop(rows, row_copy(i, slot), "wait")
        x = buf_ref[slot]
        o_ref[...] = (x * _rms_scale(x) * gain_ref[...]).astype(BF16)


def _gather_norm_rows(h, gain, src, n_rows_used, rows):
    p = src.shape[0]
    d = h.shape[1]
    return pl.pallas_call(
        functools.partial(_gather_body, rows=rows),
        grid_spec=pltpu.PrefetchScalarGridSpec(
            num_scalar_prefetch=2,
            grid=(p // rows,),
            in_specs=[pl.BlockSpec((1, d), lambda i, src, nr: (0, 0)), pl.BlockSpec(memory_space=pl.ANY)],
            out_specs=pl.BlockSpec((rows, d), lambda i, src, nr: (i, 0)),
            scratch_shapes=[pltpu.VMEM((2, rows, d), F32), pltpu.SemaphoreType.DMA((2,))],
        ),
        out_shape=jax.ShapeDtypeStruct((p, d), BF16),
        compiler_params=_params(("arbitrary",), 40),
        name="moe_gather_rows",
    )(src, n_rows_used, gain.reshape(1, d), h)


def _expert_changed(pf, n, m):
    return pf[2][m] == 1


def _tile_valid(pf, n, m):
    return pf[3][m] == 1


def _moe_gate_up(xs, w_gate_up, plan, tm):
    p, k = xs.shape
    dff = w_gate_up.shape[2] // 2
    tn = _tile(dff, 512, LANES)
    nt = dff // tn

    def epilogue(acc, e_refs, o_refs, pf, n, mi):
        o_refs[0][...] = _silu_mul(acc, tn).astype(BF16)

    return _ws_matmul(
        xs, [w_gate_up, w_gate_up], [],
        grid=(nt, p // tm),
        x_spec=pl.BlockSpec((tm, k), lambda n, mi, te, tr, tf, tv: (tr[mi], 0)),
        w_specs=[pl.BlockSpec((None, k, tn), lambda n, mi, te, tr, tf, tv: (te[mi], 0, n)),
                 pl.BlockSpec((None, k, tn), lambda n, mi, te, tr, tf, tv: (te[mi], 0, n + nt))],
        extra_specs=[],
        out_shape=jax.ShapeDtypeStruct((p, dff), BF16),
        out_specs=pl.BlockSpec((tm, tn), lambda n, mi, te, tr, tf, tv: (mi, n)),
        epilogue=epilogue, tn=tn, prefetch=plan, recast=_expert_changed, valid=_tile_valid,
        vmem_mib=56, name="moe_gate_up")


def _moe_down(hs, w_down, plan, tm):
    p, k = hs.shape
    d = w_down.shape[2]
    tn = _tile(d, 1024, LANES)

    def epilogue(acc, e_refs, o_refs, pf, n, mi):
        o_refs[0][...] = acc

    return _ws_matmul(
        hs, [w_down], [],
        grid=(d // tn, p // tm),
        x_spec=pl.BlockSpec((tm, k), lambda n, mi, te, tr, tf, tv: (tr[mi], 0)),
        w_specs=[pl.BlockSpec((None, k, tn), lambda n, mi, te, tr, tf, tv: (te[mi], 0, n))],
        extra_specs=[],
        out_shape=jax.ShapeDtypeStruct((p, d), F32),
        out_specs=pl.BlockSpec((tm, tn), lambda n, mi, te, tr, tf, tv: (mi, n)),
        epilogue=epilogue, tn=tn, prefetch=plan, recast=_expert_changed, valid=_tile_valid,
        vmem_mib=56, name="moe_down")


def _combine_body(pos_ref, h_ref, gate_ref, gain_ref, ys_hbm, h_out_ref, hn_ref, buf_ref, sems, *, tm):
    i = pl.program_id(0)
    slot = i % 2

    def row_copy(t, s):
        def copy(q):
            j, c = q // TOP_K, q % TOP_K
            return pltpu.make_async_copy(ys_hbm.at[pl.ds(pos_ref[t * tm * TOP_K + q], 1)],
                                         buf_ref.at[s * TOP_K + c, pl.ds(j, 1)], sems.at[s])
        return copy

    @pl.when(i == 0)
    def _():
        _row_dma_loop(tm * TOP_K, row_copy(0, 0), "start")

    @pl.when(i + 1 < pl.num_programs(0))
    def _():
        _row_dma_loop(tm * TOP_K, row_copy(i + 1, 1 - slot), "start")

    _row_dma_loop(tm * TOP_K, row_copy(i, slot), "wait")
    gates = gate_ref[...]
    h = h_ref[...]
    for c in range(TOP_K):
        h = h + gates[:, c:c + 1] * buf_ref[slot * TOP_K + c]
    h_out_ref[...] = h
    hn_ref[...] = (h * _rms_scale(h) * gain_ref[...]).astype(BF16)


def _moe_combine(h, ys, pos, gates, next_gain):
    m, d = h.shape
    tm = _tile(m, 256, 8)
    row = pl.BlockSpec((tm, d), lambda i, pos: (i, 0))
    return pl.pallas_call(
        functools.partial(_combine_body, tm=tm),
        grid_spec=pltpu.PrefetchScalarGridSpec(
            num_scalar_prefetch=1,
            grid=(m // tm,),
            in_specs=[row, pl.BlockSpec((tm, LANES), lambda i, pos: (i, 0)),
                      pl.BlockSpec((1, d), lambda i, pos: (0, 0)), pl.BlockSpec(memory_space=pl.ANY)],
            out_specs=[row, row],
            scratch_shapes=[pltpu.VMEM((2 * TOP_K, tm, d), F32), pltpu.SemaphoreType.DMA((2,))],
        ),
        out_shape=[jax.ShapeDtypeStruct((m, d), F32), jax.ShapeDtypeStruct((m, d), BF16)],
        compiler_params=_params(("arbitrary",), 48),
        name="moe_combine",
    )(pos.reshape(-1), h, gates, next_gain.reshape(1, d), ys)


def kernel(x, p, pool_norm, pool_w, pool_scale, kv_norm, w_kv, k_norm, attn_norm, w_q, q_norm, w_o,
           ffn_norm, dense_w_gate_up, dense_w_down, moe_router, moe_w_gate_up, moe_w_down, ple_w,
           ple_norm, ple_gate_w):
    batch, seq, d = x.shape
    m = batch * seq
    p_bf16 = p.astype(BF16).reshape(p.shape[0], m, p.shape[-1])
    cos, sin_signed = _rope_tables(seq)

    h, hn = _pool_layer(x.reshape(m, d), seq, pool_norm[0], pool_w[0], pool_scale[0], ffn_norm[0])
    act = _gate_up(hn, dense_w_gate_up[0])
    h = _matmul_residual(act, dense_w_down[0], h, name="dense_down")
    (hn,) = _rms_cast(h, [ple_norm[0]])
    h = _ple(hn, ple_gate_w[0], p_bf16[0], ple_w[0], h)

    hn_kv, hn_q = _rms_cast(h, [kv_norm, attn_norm[0]])
    width = w_q.shape[2]
    kv = _qk_projection(hn_kv, w_kv, k_norm, cos, sin_signed, seq, n_rope_cols=width,
                        post_scale=1.0, name="kv_proj")
    q = _qk_projection(hn_q, w_q[0], q_norm[0], cos, sin_signed, seq, n_rope_cols=width,
                       post_scale=1.0 / math.sqrt(HEAD_DIM), name="q_proj")

    attn = _dilated_attention(q, kv, batch=batch, seq=seq)
    h = _matmul_residual(attn, w_o[0], h, name="attn_out_proj")

    top_idx, gates = _router(h, ffn_norm[1], moe_router[0])
    tm_moe = _tile(m, 512, 8)
    src, pos, tile_expert, tile_row, tile_first, tile_valid, n_rows_used = _dispatch_plan(
        top_idx[:, :TOP_K], tm_moe)
    plan = (tile_expert, tile_row, tile_first, tile_valid)
    xs = _gather_norm_rows(h, ffn_norm[1], src, n_rows_used, tm_moe)
    hs = _moe_gate_up(xs, moe_w_gate_up[0], plan, tm_moe)
    ys = _moe_down(hs, moe_w_down[0], plan, tm_moe)
    h, hn = _moe_combine(h, ys, pos, gates, ple_norm[1])
    h = _ple(hn, ple_gate_w[1], p_bf16[1], ple_w[1], h)
    return h.reshape(batch, seq, d)
```

```python
import functools
import math

import jax
import jax.numpy as jnp
from jax import lax
from jax.experimental import pallas as pl
from jax.experimental.pallas import tpu as pltpu

F32 = jnp.float32
BF16 = jnp.bfloat16

HEAD_DIM = 128
LANES = 128
ROPE_THETA = 10000.0
RMS_EPS = 1e-6
POOL_WINDOWS = (2, 4, 8, 16)
POOL_HALO = 32
DILATED_BRANCHES = ((128, 1), (512, 4), (2048, 16))
ATTN_BLOCK = 128
N_EXPERTS = 8
TOP_K = 2
NEG = -1e30
MIB = 1024 * 1024


def _params(semantics, vmem_mib):
    return pltpu.CompilerParams(dimension_semantics=semantics, vmem_limit_bytes=vmem_mib * MIB)


def _tile(dim, pref, quantum):
    if dim <= pref:
        return dim
    t = (pref // quantum) * quantum
    while t > quantum and dim % t:
        t -= quantum
    assert dim % t == 0, (dim, pref, quantum)
    return t


def _rms_scale(x):
    return lax.rsqrt(jnp.mean(x * x, axis=-1, keepdims=True) + RMS_EPS)


def _pool_body(x_ref, halo_ref, xcol_ref, w_ref, gain_ref, scale_ref, next_gain_ref,
               h_ref, hn_ref, yn_ref, hrow_ref, lvl_ref, *, ts, seq, n_groups, gdim):
    i = pl.program_id(0)
    g = pl.program_id(1)
    t0 = (i * ts) % seq

    @pl.when(g == 0)
    def _():
        x = x_ref[...]
        y = x * _rms_scale(x) * gain_ref[...]
        xh = halo_ref[...]
        yh = xh * _rms_scale(xh) * gain_ref[...]
        yh = yh * (t0 > 0).astype(F32)
        for gi in range(n_groups):
            yn_ref[gi, :POOL_HALO, :] = yh[:, gi * gdim:(gi + 1) * gdim]
            yn_ref[gi, POOL_HALO:, :] = y[:, gi * gdim:(gi + 1) * gdim]

    pos = t0 + lax.broadcasted_iota(jnp.int32, (ts, 1), 0)
    for gi, win in enumerate(POOL_WINDOWS):
        @pl.when(g == gi)
        def _():
            levels = win.bit_length() - 1
            assert win == 1 << levels and 8 * levels <= POOL_HALO
            src = yn_ref.at[gi]
            for lv in range(1, levels + 1):
                shift, lo = 1 << (lv - 1), 8 * lv
                dst = lvl_ref.at[lv % 2]
                dst[lo:, :] = src[lo:, :] + src[lo - shift:ts + POOL_HALO - shift, :]
                src = dst
            cur = yn_ref[gi, POOL_HALO:, :]
            acc = src[POOL_HALO:, :]
            inv_cnt = 1.0 / jnp.minimum(pos + 1, win).astype(F32)
            pooled = (acc * inv_cnt - cur).astype(BF16)
            mixed = jnp.dot(pooled, w_ref[...].astype(BF16), preferred_element_type=F32)
            hrow_ref[gi] = xcol_ref[...] + mixed * scale_ref[...]

    h_ref[...] = hrow_ref[g]

    @pl.when(g == n_groups - 1)
    def _():
        ss = jnp.zeros((ts, 1), F32)
        for gi in range(n_groups):
            hg = hrow_ref[gi]
            ss = ss + jnp.sum(hg * hg, axis=-1, keepdims=True)
        inv = lax.rsqrt(ss / (n_groups * gdim) + RMS_EPS)
        for gi in range(n_groups):
            sl = slice(gi * gdim, (gi + 1) * gdim)
            hn_ref[:, sl] = (hrow_ref[gi] * inv * next_gain_ref[:, sl]).astype(BF16)


def _pool_layer(x2, seq, norm_gain, w_groups, scale, next_gain):
    m, d = x2.shape
    n_groups, gdim = w_groups.shape[0], w_groups.shape[1]
    assert n_groups == len(POOL_WINDOWS) and n_groups * gdim == d
    ts = _tile(seq, 256, POOL_HALO)
    hpt = ts // POOL_HALO
    body = functools.partial(_pool_body, ts=ts, seq=seq, n_groups=n_groups, gdim=gdim)
    return pl.pallas_call(
        body,
        grid=(m // ts, n_groups),
        in_specs=[
            pl.BlockSpec((ts, d), lambda i, g: (i, 0)),
            pl.BlockSpec((POOL_HALO, d), lambda i, g: (jnp.maximum(i * hpt - 1, 0), 0)),
            pl.BlockSpec((ts, gdim), lambda i, g: (i, g)),
            pl.BlockSpec((None, gdim, gdim), lambda i, g: (g, 0, 0)),
            pl.BlockSpec((1, d), lambda i, g: (0, 0)),
            pl.BlockSpec((1, gdim), lambda i, g: (0, g)),
            pl.BlockSpec((1, d), lambda i, g: (0, 0)),
        ],
        out_specs=[
            pl.BlockSpec((ts, gdim), lambda i, g: (i, g)),
            pl.BlockSpec((ts, d), lambda i, g: (i, 0)),
        ],
        out_shape=[jax.ShapeDtypeStruct((m, d), F32), jax.ShapeDtypeStruct((m, d), BF16)],
        scratch_shapes=[
            pltpu.VMEM((n_groups, ts + POOL_HALO, gdim), F32),
            pltpu.VMEM((n_groups, ts, gdim), F32),
            pltpu.VMEM((2, ts + POOL_HALO, gdim), F32),
        ],
        compiler_params=_params(("arbitrary", "arbitrary"), 48),
        name="pool_layer",
    )(x2, x2, x2, w_groups, norm_gain.reshape(1, d), scale.reshape(1, d), next_gain.reshape(1, d))


def _rms_body(x_ref, *refs, n_out):
    gains, outs = refs[:n_out], refs[n_out:]
    x = x_ref[...]
    y = x * _rms_scale(x)
    for g_ref, o_ref in zip(gains, outs):
        o_ref[...] = (y * g_ref[...]).astype(BF16)


def _rms_cast(x2, gains):
    m, d = x2.shape
    tm = _tile(m, 256, 8)
    n_out = len(gains)
    row = pl.BlockSpec((tm, d), lambda i: (i, 0))
    vec = pl.BlockSpec((1, d), lambda i: (0, 0))
    return pl.pallas_call(
        functools.partial(_rms_body, n_out=n_out),
        grid=(m // tm,),
        in_specs=[row] + [vec] * n_out,
        out_specs=[row] * n_out,
        out_shape=[jax.ShapeDtypeStruct((m, d), BF16)] * n_out,
        compiler_params=_params(("arbitrary",), 32),
        name="rms_cast",
    )(x2, *[g.reshape(1, d) for g in gains])


def _ws_matmul(x, ws, extras, *, grid, x_spec, w_specs, extra_specs, out_shape, out_specs,
               epilogue, tn, recast, k_block=None, prefetch=(), valid=None, vmem_mib=48,
               name="ws_matmul"):
    n_w, n_e, n_pf = len(ws), len(extras), len(prefetch)
    outs = out_shape if isinstance(out_shape, (list, tuple)) else [out_shape]
    n_out = len(outs)
    k = x.shape[1] if k_block is None else k_block
    cast_rows = _tile(k, 512, 16)

    def body(*refs):
        pf, refs = refs[:n_pf], refs[n_pf:]
        x_ref = refs[0]
        w_refs = refs[1:1 + n_w]
        e_refs = refs[1 + n_w:1 + n_w + n_e]
        o_refs = refs[1 + n_w + n_e:1 + n_w + n_e + n_out]
        wb_ref = refs[1 + n_w + n_e + n_out]
        n, m = pl.program_id(0), pl.program_id(1)

        def compute():
            @pl.when(recast(pf, n, m))
            def _():
                def chunk(c, carry):
                    r0 = pl.multiple_of(c * cast_rows, cast_rows)
                    for j, w_ref in enumerate(w_refs):
                        wb_ref[pl.ds(r0, cast_rows), j * tn:(j + 1) * tn] = (
                            w_ref[pl.ds(r0, cast_rows), :].astype(BF16))
                    return carry
                lax.fori_loop(0, k // cast_rows, chunk, 0)

            acc = jnp.dot(x_ref[...], wb_ref[...], preferred_element_type=F32)
            epilogue(acc, e_refs, o_refs, pf, n, m)

        if valid is None:
            compute()
        else:
            ok = valid(pf, n, m)
            pl.when(ok)(compute)

            @pl.when(jnp.logical_not(ok))
            def _():
                for o_ref in o_refs:
                    o_ref[...] = jnp.zeros(o_ref.shape, o_ref.dtype)

    return pl.pallas_call(
        body,
        grid_spec=pltpu.PrefetchScalarGridSpec(
            num_scalar_prefetch=n_pf,
            grid=grid,
            in_specs=[x_spec] + list(w_specs) + list(extra_specs),
            out_specs=out_specs,
            scratch_shapes=[pltpu.VMEM((k, n_w * tn), BF16)],
        ),
        out_shape=out_shape,
        compiler_params=_params(("arbitrary", "arbitrary"), vmem_mib),
        name=name,
    )(*prefetch, x, *ws, *extras)


def _first_row_tile(pf, n, m):
    return m == 0


def _silu_mul(acc, tn):
    gate, up = acc[:, :tn], acc[:, tn:]
    return gate * jax.nn.sigmoid(gate) * up


def _gate_up(xn, w_gate_up):
    m, k = xn.shape
    dff = w_gate_up.shape[1] // 2
    tm, tn = _tile(m, 1024, 8), _tile(dff, 256, LANES)
    nt = dff // tn

    def epilogue(acc, e_refs, o_refs, pf, n, mi):
        o_refs[0][...] = _silu_mul(acc, tn).astype(BF16)

    return _ws_matmul(
        xn, [w_gate_up, w_gate_up], [],
        grid=(nt, m // tm),
        x_spec=pl.BlockSpec((tm, k), lambda n, mi: (mi, 0)),
        w_specs=[pl.BlockSpec((k, tn), lambda n, mi: (0, n)),
                 pl.BlockSpec((k, tn), lambda n, mi: (0, n + nt))],
        extra_specs=[],
        out_shape=jax.ShapeDtypeStruct((m, dff), BF16),
        out_specs=pl.BlockSpec((tm, tn), lambda n, mi: (mi, n)),
        epilogue=epilogue, tn=tn, recast=_first_row_tile, name="dense_gate_up")


MAX_WEIGHT_TILE_ROWS = 6144


def _matmul_residual(x, w, res, *, name):
    m, k = x.shape
    n_out = w.shape[1]
    tm, tn = _tile(m, 512, 8), _tile(n_out, 512, LANES)
    n_chunks = pl.cdiv(k, MAX_WEIGHT_TILE_ROWS)
    assert k % (n_chunks * LANES) == 0
    kc = k // n_chunks

    def epilogue(acc, e_refs, o_refs, pf, n, mi):
        o_refs[0][...] = e_refs[0][...] + acc

    for c in range(n_chunks):
        res = _ws_matmul(
            x, [w], [res],
            grid=(n_out // tn, m // tm),
            x_spec=pl.BlockSpec((tm, kc), lambda n, mi, c=c: (mi, c)),
            w_specs=[pl.BlockSpec((kc, tn), lambda n, mi, c=c: (c, n))],
            extra_specs=[pl.BlockSpec((tm, tn), lambda n, mi: (mi, n))],
            out_shape=jax.ShapeDtypeStruct((m, n_out), F32),
            out_specs=pl.BlockSpec((tm, tn), lambda n, mi: (mi, n)),
            epilogue=epilogue, tn=tn, k_block=kc, recast=_first_row_tile, vmem_mib=56,
            name=name)
    return res


def _ple(hn, w_gate, p_bf16, w_ple, h, layer):
    m, k = hn.shape
    d = w_gate.shape[2]
    pdim = p_bf16.shape[2]
    tm, tn = _tile(m, 1024, 8), _tile(d, 512, LANES)

    def epilogue(acc, e_refs, o_refs, pf, n, mi):
        p_ref, wp_ref, h_ref = e_refs
        emb = jnp.dot(p_ref[...], wp_ref[...].astype(BF16), preferred_element_type=F32)
        o_refs[0][...] = h_ref[...] + emb * jax.nn.sigmoid(acc)

    return _ws_matmul(
        hn, [w_gate], [p_bf16, w_ple, h],
        grid=(d // tn, m // tm),
        x_spec=pl.BlockSpec((tm, k), lambda n, mi: (mi, 0)),
        w_specs=[pl.BlockSpec((None, k, tn), lambda n, mi: (layer, 0, n))],
        extra_specs=[pl.BlockSpec((None, tm, pdim), lambda n, mi: (layer, mi, 0)),
                     pl.BlockSpec((None, pdim, tn), lambda n, mi: (layer, 0, n)),
                     pl.BlockSpec((tm, tn), lambda n, mi: (mi, n))],
        out_shape=jax.ShapeDtypeStruct((m, d), F32),
        out_specs=pl.BlockSpec((tm, tn), lambda n, mi: (mi, n)),
        epilogue=epilogue, tn=tn, recast=_first_row_tile, vmem_mib=56, name="ple")


def _rope_tables(seq):
    half = HEAD_DIM // 2
    inv_freq = jnp.exp(-math.log(ROPE_THETA) * jnp.arange(half, dtype=F32) / half)
    ang = jnp.arange(seq, dtype=F32)[:, None] * inv_freq[None, :]
    cos, sin = jnp.cos(ang), jnp.sin(ang)
    return jnp.concatenate([cos, cos], axis=-1), jnp.concatenate([-sin, sin], axis=-1)


def _norm_rope_heads(acc, gain, cos, sin_signed, out_ref, tn, post_scale):
    for hh in range(tn // HEAD_DIM):
        sl = slice(hh * HEAD_DIM, (hh + 1) * HEAD_DIM)
        y = acc[:, sl]
        y = y * _rms_scale(y) * gain
        y = y * cos + pltpu.roll(y, HEAD_DIM // 2, axis=1) * sin_signed
        if post_scale != 1.0:
            y = y * post_scale
        out_ref[:, sl] = y.astype(out_ref.dtype)


def _qk_projection(xn, w, head_gain, cos, sin_signed, seq, *, n_rope_cols, post_scale, name):
    m, k = xn.shape
    n_out = w.shape[1]
    tm, tn = _tile(seq, 1024, 8), _tile(min(n_out, n_rope_cols), 512, LANES)
    n_rope_tiles = n_rope_cols // tn
    spb = seq // tm

    def epilogue(acc, e_refs, o_refs, pf, n, mi):
        g_ref, c_ref, s_ref = e_refs

        @pl.when(n < n_rope_tiles)
        def _():
            _norm_rope_heads(acc, g_ref[...], c_ref[...], s_ref[...], o_refs[0], tn, post_scale)

        if n_rope_tiles * tn < n_out:
            @pl.when(n >= n_rope_tiles)
            def _():
                o_refs[0][...] = acc.astype(BF16)

    return _ws_matmul(
        xn, [w], [head_gain.reshape(1, HEAD_DIM), cos, sin_signed],
        grid=(n_out // tn, m // tm),
        x_spec=pl.BlockSpec((tm, k), lambda n, mi: (mi, 0)),
        w_specs=[pl.BlockSpec((k, tn), lambda n, mi: (0, n))],
        extra_specs=[pl.BlockSpec((1, HEAD_DIM), lambda n, mi: (0, 0)),
                     pl.BlockSpec((tm, HEAD_DIM), lambda n, mi: (mi % spb, 0)),
                     pl.BlockSpec((tm, HEAD_DIM), lambda n, mi: (mi % spb, 0))],
        out_shape=jax.ShapeDtypeStruct((m, n_out), BF16),
        out_specs=pl.BlockSpec((tm, tn), lambda n, mi: (mi, n)),
        epilogue=epilogue, tn=tn, recast=_first_row_tile, vmem_mib=56, name=name)


def _attn_body(q_ref, k_ref, v_ref, o_ref, qs_ref, ks_ref, vs_ref, acc_ref, m_ref, l_ref,
               s_ref, p_ref, a_ref, *, seq):
    blk = ATTN_BLOCK
    regroup = max(d for _, d in DILATED_BRANCHES)
    res_rows = seq // regroup
    row = lax.broadcasted_iota(jnp.int32, (blk, blk), 0)
    col = lax.broadcasted_iota(jnp.int32, (blk, blk), 1)
    contract_last = (((1,), (1,)), ((), ()))
    order = sorted(DILATED_BRANCHES, key=lambda wd: -wd[1])
    assert order[-1][1] == 1 and regroup % 8 == 0 and blk % regroup == 0

    def seq_pos(idx, pieces):
        n = blk // pieces
        return pieces * (idx % n) + idx // n

    def pieces_of(d, c, i):
        n = blk * d // regroup
        return [((c + d * a) * res_rows + n * i, n) for a in range(regroup // d)]

    def load(ref, pieces):
        parts = [ref[pl.ds(s, n), :] for s, n in pieces]
        return parts[0] if len(parts) == 1 else jnp.concatenate(parts, axis=0)

    def store(ref, pieces, val):
        off = 0
        for s, n in pieces:
            ref[pl.ds(s, n), :] = val[off:off + n]
            off += n

    gather_rows = (col == seq_pos(row, regroup)).astype(BF16)
    scatter_rows = (row == seq_pos(col, regroup)).astype(BF16)
    for i in range(seq // blk):
        for src, dst in ((q_ref, qs_ref), (k_ref, ks_ref), (v_ref, vs_ref)):
            regrouped = jnp.dot(gather_rows, src[pl.ds(i * blk, blk), :], preferred_element_type=F32)
            store(dst, pieces_of(1, 0, i), regrouped)

    for step, (window, d) in enumerate(order):
        first, last = step == 0, step == len(order) - 1
        pieces = regroup // d
        pos_r, pos_c = seq_pos(row, pieces), seq_pos(col, pieces)
        bias_cur = jnp.where(pos_c <= pos_r, 0.0, NEG).astype(F32)
        bias_prev = jnp.where(pos_c >= pos_r, 0.0, NEG).astype(F32)
        tiles = [(pieces_of(d, c, i), pieces_of(d, c, i - 1) if i > 0 else None, i)
                 for c in range(d) for i in range(seq // d // blk)]

        for t, (cur, prev, i) in enumerate(tiles):
            q = load(qs_ref, cur).astype(BF16)
            s_ref[t, :, :blk] = lax.dot_general(q, load(ks_ref, cur).astype(BF16), contract_last,
                                                preferred_element_type=F32) + bias_cur
            if prev is not None:
                s_ref[t, :, blk:] = lax.dot_general(q, load(ks_ref, prev).astype(BF16), contract_last,
                                                    preferred_element_type=F32) + bias_prev

        for t, (cur, prev, i) in enumerate(tiles):
            s_c = s_ref[t, :, :blk]
            mx = jnp.max(s_c, axis=-1, keepdims=True)
            if prev is not None:
                s_p = s_ref[t, :, blk:]
                mx = jnp.maximum(mx, jnp.max(s_p, axis=-1, keepdims=True))
            if first:
                m_new = jnp.broadcast_to(mx, (blk, LANES))
            else:
                m_old = load(m_ref, cur)
                m_new = jnp.maximum(m_old, mx)
            p_c = jnp.exp(s_c - m_new)
            den = jnp.sum(p_c, axis=-1, keepdims=True)
            p_ref[t, :, :blk] = p_c.astype(BF16)
            if prev is not None:
                p_p = jnp.exp(s_p - m_new)
                den = den + jnp.sum(p_p, axis=-1, keepdims=True)
                p_ref[t, :, blk:] = p_p.astype(BF16)
            if first:
                den = jnp.broadcast_to(den, (blk, LANES))
            else:
                alpha = jnp.exp(m_old - m_new)
                a_ref[t] = alpha
                den = alpha * load(l_ref, cur) + den
            store(l_ref, cur, den)
            if not last:
                store(m_ref, cur, m_new)

        for t, (cur, prev, i) in enumerate(tiles):
            acc = jnp.dot(p_ref[t, :, :blk], load(vs_ref, cur).astype(BF16), preferred_element_type=F32)
            if prev is not None:
                acc = acc + jnp.dot(p_ref[t, :, blk:], load(vs_ref, prev).astype(BF16),
                                    preferred_element_type=F32)
            if not first:
                acc = a_ref[t] * load(acc_ref, cur) + acc
            if last:
                out = jnp.dot(scatter_rows, (acc / load(l_ref, cur)).astype(BF16),
                              preferred_element_type=F32)
                o_ref[pl.ds(i * blk, blk), :] = out.astype(o_ref.dtype)
            else:
                store(acc_ref, cur, acc)


def _dilated_attention(q, kv, *, batch, seq):
    m, width = q.shape
    assert ATTN_BLOCK == LANES
    for window, d in DILATED_BRANCHES:
        assert window // d == ATTN_BLOCK and seq % (d * ATTN_BLOCK) == 0
    n_heads = width // HEAD_DIM
    head_block = pl.BlockSpec((seq, HEAD_DIM), lambda b, h: (b, h))
    return pl.pallas_call(
        functools.partial(_attn_body, seq=seq),
        grid=(batch, n_heads),
        in_specs=[head_block, head_block, pl.BlockSpec((seq, HEAD_DIM), lambda b, h: (b, n_heads + h))],
        out_specs=head_block,
        out_shape=jax.ShapeDtypeStruct((m, width), BF16),
        scratch_shapes=[pltpu.VMEM((seq, LANES), F32)] * 6 + [
            pltpu.VMEM((seq // ATTN_BLOCK, ATTN_BLOCK, 2 * ATTN_BLOCK), F32),
            pltpu.VMEM((seq // ATTN_BLOCK, ATTN_BLOCK, 2 * ATTN_BLOCK), BF16),
            pltpu.VMEM((seq // ATTN_BLOCK, ATTN_BLOCK, LANES), F32)],

        compiler_params=_params(("arbitrary", "arbitrary"), 32),
        name="dilated_attention",
    )(q, kv, kv)


def _router_body(h_ref, gain_ref, r_ref, idx_ref, gate_ref):
    x = h_ref[...]
    hn = x * _rms_scale(x) * gain_ref[...]
    logits = jnp.dot(hn, r_ref[...], preferred_element_type=F32, precision=lax.Precision.HIGHEST)
    lane = lax.broadcasted_iota(jnp.int32, logits.shape, 1).astype(F32)
    logits = jnp.where(lane < N_EXPERTS, logits, NEG)
    m1 = jnp.max(logits, axis=-1, keepdims=True)
    i1 = jnp.min(jnp.where(logits == m1, lane, float(LANES)), axis=-1, keepdims=True)
    rest = jnp.where(lane == i1, NEG, logits)
    m2 = jnp.max(rest, axis=-1, keepdims=True)
    i2 = jnp.min(jnp.where(rest == m2, lane, float(LANES)), axis=-1, keepdims=True)
    e2 = jnp.exp(m2 - m1)
    g1 = 1.0 / (1.0 + e2)
    g2 = e2 / (1.0 + e2)
    idx_ref[...] = jnp.where(lane == 0, i1, jnp.where(lane == 1, i2, 0.0)).astype(jnp.int32)
    gate_ref[...] = jnp.where(lane == 0, g1, jnp.where(lane == 1, g2, 0.0))


def _router(h, gain, router):
    m, d = h.shape
    n_e = router.shape[1]
    assert n_e == N_EXPERTS
    tm = _tile(m, 256, 8)
    r_pad = jnp.zeros((d, LANES), F32).at[:, :n_e].set(router)
    row = pl.BlockSpec((tm, d), lambda i: (i, 0))
    small = pl.BlockSpec((tm, LANES), lambda i: (i, 0))
    return pl.pallas_call(
        _router_body,
        grid=(m // tm,),
        in_specs=[row, pl.BlockSpec((1, d), lambda i: (0, 0)), pl.BlockSpec((d, LANES), lambda i: (0, 0))],
        out_specs=[small, small],
        out_shape=[jax.ShapeDtypeStruct((m, LANES), jnp.int32), jax.ShapeDtypeStruct((m, LANES), F32)],
        compiler_params=_params(("arbitrary",), 32),
        name="router_top2",
    )(h, gain.reshape(1, d), r_pad)


def _dispatch_plan(top_idx, tm):
    m = top_idx.shape[0]
    n_tiles = (m * TOP_K + N_EXPERTS * (tm - 1)) // tm
    flat_e = top_idx.reshape(-1)
    onehot = (flat_e[:, None] == jnp.arange(N_EXPERTS)[None, :]).astype(jnp.int32)
    rank = jnp.sum((jnp.cumsum(onehot, axis=0) - 1) * onehot, axis=1)
    counts = jnp.sum(onehot, axis=0)
    tiles_per = (counts + tm - 1) // tm
    tile_end = jnp.cumsum(tiles_per)
    row_start = (tile_end - tiles_per) * tm
    dest = row_start[flat_e] + rank
    src = jnp.zeros((n_tiles * tm,), jnp.int32).at[dest].set(jnp.arange(m * TOP_K, dtype=jnp.int32) // TOP_K)
    n_used = tile_end[-1]
    t = jnp.minimum(jnp.arange(n_tiles, dtype=jnp.int32), n_used - 1)
    tile_expert = jnp.sum((t[:, None] >= tile_end[None, :]).astype(jnp.int32), axis=1)
    tile_first = (t == (tile_end - tiles_per)[tile_expert]).astype(jnp.int32)
    tile_valid = (jnp.arange(n_tiles) < n_used).astype(jnp.int32)
    return (src, dest.reshape(m, TOP_K).astype(jnp.int32), tile_expert.astype(jnp.int32), t,
            tile_first, tile_valid, (n_used * tm).astype(jnp.int32).reshape(1))


DMA_LOOP_UNROLL = 8


def _row_dma_loop(n, copy_of, action):
    def step(j, c):
        getattr(copy_of(j), action)()
        return c
    lax.fori_loop(0, n, step, 0, unroll=DMA_LOOP_UNROLL)


def _gather_body(src_ref, nrows_ref, gain_ref, h_hbm, o_ref, buf_ref, sems, *, rows):
    i = pl.program_id(0)
    slot = i % 2

    def tile_used(t):
        return t * rows < nrows_ref[0]

    def row_copy(t, s):
        return lambda j: pltpu.make_async_copy(
            h_hbm.at[pl.ds(src_ref[t * rows + j], 1)], buf_ref.at[s, pl.ds(j, 1)], sems.at[s])

    @pl.when((i == 0) & tile_used(0))
    def _():
        _row_dma_loop(rows, row_copy(0, 0), "start")

    @pl.when((i + 1 < pl.num_programs(0)) & tile_used(i + 1))
    def _():
        _row_dma_loop(rows, row_copy(i + 1, 1 - slot), "start")

    @pl.when(jnp.logical_not(tile_used(i)))
    def _():
        o_ref[...] = jnp.zeros(o_ref.shape, o_ref.dtype)

    @pl.when(tile_used(i))
    def _():
        _row_dma_loop(rows, row_copy(i, slot), "wait")
        x = buf_ref[slot]
        o_ref[...] = (x * _rms_scale(x) * gain_ref[...]).astype(BF16)


def _gather_norm_rows(h, gain, src, n_rows_used, rows):
    p = src.shape[0]
    d = h.shape[1]
    return pl.pallas_call(
        functools.partial(_gather_body, rows=rows),
        grid_spec=pltpu.PrefetchScalarGridSpec(
            num_scalar_prefetch=2,
            grid=(p // rows,),
            in_specs=[pl.BlockSpec((1, d), lambda i, src, nr: (0, 0)), pl.BlockSpec(memory_space=pl.ANY)],
            out_specs=pl.BlockSpec((rows, d), lambda i, src, nr: (i, 0)),
            scratch_shapes=[pltpu.VMEM((2, rows, d), F32), pltpu.SemaphoreType.DMA((2,))],
        ),
        out_shape=jax.ShapeDtypeStruct((p, d), BF16),
        compiler_params=_params(("arbitrary",), 40),
        name="moe_gather_rows",
    )(src, n_rows_used, gain.reshape(1, d), h)


def _expert_changed(pf, n, m):
    return pf[2][m] == 1


def _tile_valid(pf, n, m):
    return pf[3][m] == 1


def _moe_gate_up(xs, w_gate_up, plan, tm):
    p, k = xs.shape
    dff = w_gate_up.shape[2] // 2
    tn = _tile(dff, 512, LANES)
    nt = dff // tn

    def epilogue(acc, e_refs, o_refs, pf, n, mi):
        o_refs[0][...] = _silu_mul(acc, tn).astype(BF16)

    return _ws_matmul(
        xs, [w_gate_up, w_gate_up], [],
        grid=(nt, p // tm),
        x_spec=pl.BlockSpec((tm, k), lambda n, mi, te, tr, tf, tv: (tr[mi], 0)),
        w_specs=[pl.BlockSpec((None, k, tn), lambda n, mi, te, tr, tf, tv: (te[mi], 0, n)),
                 pl.BlockSpec((None, k, tn), lambda n, mi, te, tr, tf, tv: (te[mi], 0, n + nt))],
        extra_specs=[],
        out_shape=jax.ShapeDtypeStruct((p, dff), BF16),
        out_specs=pl.BlockSpec((tm, tn), lambda n, mi, te, tr, tf, tv: (mi, n)),
        epilogue=epilogue, tn=tn, prefetch=plan, recast=_expert_changed, valid=_tile_valid,
        vmem_mib=56, name="moe_gate_up")


def _moe_down(hs, w_down, plan, tm):
    p, k = hs.shape
    d = w_down.shape[2]
    tn = _tile(d, 1024, LANES)

    def epilogue(acc, e_refs, o_refs, pf, n, mi):
        o_refs[0][...] = acc

    return _ws_matmul(
        hs, [w_down], [],
        grid=(d // tn, p // tm),
        x_spec=pl.BlockSpec((tm, k), lambda n, mi, te, tr, tf, tv: (tr[mi], 0)),
        w_specs=[pl.BlockSpec((None, k, tn), lambda n, mi, te, tr, tf, tv: (te[mi], 0, n))],
        extra_specs=[],
        out_shape=jax.ShapeDtypeStruct((p, d), F32),
        out_specs=pl.BlockSpec((tm, tn), lambda n, mi, te, tr, tf, tv: (mi, n)),
        epilogue=epilogue, tn=tn, prefetch=plan, recast=_expert_changed, valid=_tile_valid,
        vmem_mib=56, name="moe_down")


def _combine_body(pos_ref, h_ref, gate_ref, gain_ref, ys_hbm, h_out_ref, hn_ref, buf_ref, sems, *, tm):
    i = pl.program_id(0)
    slot = i % 2

    def row_copy(t, s):
        def copy(q):
            j, c = q // TOP_K, q % TOP_K
            return pltpu.make_async_copy(ys_hbm.at[pl.ds(pos_ref[t * tm * TOP_K + q], 1)],
                                         buf_ref.at[s * TOP_K + c, pl.ds(j, 1)], sems.at[s])
        return copy

    @pl.when(i == 0)
    def _():
        _row_dma_loop(tm * TOP_K, row_copy(0, 0), "start")

    @pl.when(i + 1 < pl.num_programs(0))
    def _():
        _row_dma_loop(tm * TOP_K, row_copy(i + 1, 1 - slot), "start")

    _row_dma_loop(tm * TOP_K, row_copy(i, slot), "wait")
    gates = gate_ref[...]
    h = h_ref[...]
    for c in range(TOP_K):
        h = h + gates[:, c:c + 1] * buf_ref[slot * TOP_K + c]
    h_out_ref[...] = h
    hn_ref[...] = (h * _rms_scale(h) * gain_ref[...]).astype(BF16)


def _moe_combine(h, ys, pos, gates, next_gain):
    m, d = h.shape
    tm = _tile(m, 256, 8)
    row = pl.BlockSpec((tm, d), lambda i, pos: (i, 0))
    return pl.pallas_call(
        functools.partial(_combine_body, tm=tm),
        grid_spec=pltpu.PrefetchScalarGridSpec(
            num_scalar_prefetch=1,
            grid=(m // tm,),
            in_specs=[row, pl.BlockSpec((tm, LANES), lambda i, pos: (i, 0)),
                      pl.BlockSpec((1, d), lambda i, pos: (0, 0)), pl.BlockSpec(memory_space=pl.ANY)],
            out_specs=[row, row],
            scratch_shapes=[pltpu.VMEM((2 * TOP_K, tm, d), F32), pltpu.SemaphoreType.DMA((2,))],
        ),
        out_shape=[jax.ShapeDtypeStruct((m, d), F32), jax.ShapeDtypeStruct((m, d), BF16)],
        compiler_params=_params(("arbitrary",), 48),
        name="moe_combine",
    )(pos.reshape(-1), h, gates, next_gain.reshape(1, d), ys)


def kernel(x, p, pool_norm, pool_w, pool_scale, kv_norm, w_kv, k_norm, attn_norm, w_q, q_norm, w_o,
           ffn_norm, dense_w_gate_up, dense_w_down, moe_router, moe_w_gate_up, moe_w_down, ple_w,
           ple_norm, ple_gate_w):
    batch, seq, d = x.shape
    m = batch * seq
    p_bf16 = p.astype(BF16).reshape(p.shape[0], m, p.shape[-1])
    cos, sin_signed = _rope_tables(seq)

    h, hn = _pool_layer(x.reshape(m, d), seq, pool_norm[0], pool_w[0], pool_scale[0], ffn_norm[0])
    act = _gate_up(hn, dense_w_gate_up[0])
    h = _matmul_residual(act, dense_w_down[0], h, name="dense_down")
    (hn,) = _rms_cast(h, [ple_norm[0]])
    h = _ple(hn, ple_gate_w, p_bf16, ple_w, h, 0)

    hn_kv, hn_q = _rms_cast(h, [kv_norm, attn_norm[0]])
    width = w_q.shape[2]
    kv = _qk_projection(hn_kv, w_kv, k_norm, cos, sin_signed, seq, n_rope_cols=width,
                        post_scale=1.0, name="kv_proj")
    q = _qk_projection(hn_q, w_q[0], q_norm[0], cos, sin_signed, seq, n_rope_cols=width,
                       post_scale=1.0 / math.sqrt(HEAD_DIM), name="q_proj")

    attn = _dilated_attention(q, kv, batch=batch, seq=seq)
    h = _matmul_residual(attn, w_o[0], h, name="attn_out_proj")

    top_idx, gates = _router(h, ffn_norm[1], moe_router[0])
    tm_moe = _tile(m, 512, 8)
    src, pos, tile_expert, tile_row, tile_first, tile_valid, n_rows_used = _dispatch_plan(
        top_idx[:, :TOP_K], tm_moe)
    plan = (tile_expert, tile_row, tile_first, tile_valid)
    xs = _gather_norm_rows(h, ffn_norm[1], src, n_rows_used, tm_moe)
    hs = _moe_gate_up(xs, moe_w_gate_up[0], plan, tm_moe)
    ys = _moe_down(hs, moe_w_down[0], plan, tm_moe)
    h, hn = _moe_combine(h, ys, pos, gates, ple_norm[1])
    h = _ple(hn, ple_gate_w, p_bf16, ple_w, h, 1)
    return h.reshape(batch, seq, d)
```

```python
import functools
import math

import jax
import jax.numpy as jnp
from jax import lax
from jax.experimental import pallas as pl
from jax.experimental.pallas import tpu as pltpu

F32 = jnp.float32
BF16 = jnp.bfloat16

HEAD_DIM = 128
LANES = 128
ROPE_THETA = 10000.0
RMS_EPS = 1e-6
POOL_WINDOWS = (2, 4, 8, 16)
POOL_HALO = 32
DILATED_BRANCHES = ((128, 1), (512, 4), (2048, 16))
ATTN_BLOCK = 128
N_EXPERTS = 8
TOP_K = 2
NEG = -1e30
MIB = 1024 * 1024


def _params(semantics, vmem_mib):
    return pltpu.CompilerParams(dimension_semantics=semantics, vmem_limit_bytes=vmem_mib * MIB)


def _tile(dim, pref, quantum):
    if dim <= pref:
        return dim
    t = (pref // quantum) * quantum
    while t > quantum and dim % t:
        t -= quantum
    assert dim % t == 0, (dim, pref, quantum)
    return t


def _rms_scale(x):
    return lax.rsqrt(jnp.mean(x * x, axis=-1, keepdims=True) + RMS_EPS)


def _pool_body(x_ref, halo_ref, w_ref, gain_ref, scale_ref, next_gain_ref,
               h_ref, hn_ref, yn_ref, hrow_ref, lvl_ref, *, ts, seq, n_groups, gdim):
    i = pl.program_id(0)
    g = pl.program_id(1)
    t0 = (i * ts) % seq

    @pl.when(g == 0)
    def _():
        x = x_ref[...]
        y = x * _rms_scale(x) * gain_ref[...]
        xh = halo_ref[...]
        yh = xh * _rms_scale(xh) * gain_ref[...]
        yh = yh * (t0 > 0).astype(F32)
        for gi in range(n_groups):
            yn_ref[gi, :POOL_HALO, :] = yh[:, gi * gdim:(gi + 1) * gdim]
            yn_ref[gi, POOL_HALO:, :] = y[:, gi * gdim:(gi + 1) * gdim]

    pos = t0 + lax.broadcasted_iota(jnp.int32, (ts, 1), 0)
    for gi, win in enumerate(POOL_WINDOWS):
        @pl.when(g == gi)
        def _():
            levels = win.bit_length() - 1
            assert win == 1 << levels and 8 * levels <= POOL_HALO
            src = yn_ref.at[gi]
            for lv in range(1, levels + 1):
                shift, lo = 1 << (lv - 1), 8 * lv
                dst = lvl_ref.at[lv % 2]
                dst[lo:, :] = src[lo:, :] + src[lo - shift:ts + POOL_HALO - shift, :]
                src = dst
            cur = yn_ref[gi, POOL_HALO:, :]
            acc = src[POOL_HALO:, :]
            inv_cnt = 1.0 / jnp.minimum(pos + 1, win).astype(F32)
            pooled = (acc * inv_cnt - cur).astype(BF16)
            sl = slice(gi * gdim, (gi + 1) * gdim)
            mixed = jnp.dot(pooled, w_ref[gi], preferred_element_type=F32)
            hrow_ref[gi] = x_ref[:, sl] + mixed * scale_ref[:, sl]

    h_ref[...] = hrow_ref[g]

    @pl.when(g == n_groups - 1)
    def _():
        ss = jnp.zeros((ts, 1), F32)
        for gi in range(n_groups):
            hg = hrow_ref[gi]
            ss = ss + jnp.sum(hg * hg, axis=-1, keepdims=True)
        inv = lax.rsqrt(ss / (n_groups * gdim) + RMS_EPS)
        for gi in range(n_groups):
            sl = slice(gi * gdim, (gi + 1) * gdim)
            hn_ref[:, sl] = (hrow_ref[gi] * inv * next_gain_ref[:, sl]).astype(BF16)


def _pool_layer(x2, seq, norm_gain, w_groups, scale, next_gain):
    m, d = x2.shape
    n_groups, gdim = w_groups.shape[0], w_groups.shape[1]
    assert n_groups == len(POOL_WINDOWS) and n_groups * gdim == d
    ts = _tile(seq, 256, POOL_HALO)
    hpt = ts // POOL_HALO
    body = functools.partial(_pool_body, ts=ts, seq=seq, n_groups=n_groups, gdim=gdim)
    return pl.pallas_call(
        body,
        grid=(m // ts, n_groups),
        in_specs=[
            pl.BlockSpec((ts, d), lambda i, g: (i, 0)),
            pl.BlockSpec((POOL_HALO, d), lambda i, g: (jnp.maximum(i * hpt - 1, 0), 0)),
            pl.BlockSpec((n_groups, gdim, gdim), lambda i, g: (0, 0, 0)),
            pl.BlockSpec((1, d), lambda i, g: (0, 0)),
            pl.BlockSpec((1, d), lambda i, g: (0, 0)),
            pl.BlockSpec((1, d), lambda i, g: (0, 0)),
        ],
        out_specs=[
            pl.BlockSpec((ts, gdim), lambda i, g: (i, g)),
            pl.BlockSpec((ts, d), lambda i, g: (i, 0)),
        ],
        out_shape=[jax.ShapeDtypeStruct((m, d), F32), jax.ShapeDtypeStruct((m, d), BF16)],
        scratch_shapes=[
            pltpu.VMEM((n_groups, ts + POOL_HALO, gdim), F32),
            pltpu.VMEM((n_groups, ts, gdim), F32),
            pltpu.VMEM((2, ts + POOL_HALO, gdim), F32),
        ],
        compiler_params=_params(("arbitrary", "arbitrary"), 48),
        name="pool_layer",
    )(x2, x2, w_groups.astype(BF16), norm_gain.reshape(1, d), scale.reshape(1, d), next_gain.reshape(1, d))


def _rms_body(x_ref, *refs, n_out):
    gains, outs = refs[:n_out], refs[n_out:]
    x = x_ref[...]
    y = x * _rms_scale(x)
    for g_ref, o_ref in zip(gains, outs):
        o_ref[...] = (y * g_ref[...]).astype(BF16)


def _rms_cast(x2, gains):
    m, d = x2.shape
    tm = _tile(m, 256, 8)
    n_out = len(gains)
    row = pl.BlockSpec((tm, d), lambda i: (i, 0))
    vec = pl.BlockSpec((1, d), lambda i: (0, 0))
    return pl.pallas_call(
        functools.partial(_rms_body, n_out=n_out),
        grid=(m // tm,),
        in_specs=[row] + [vec] * n_out,
        out_specs=[row] * n_out,
        out_shape=[jax.ShapeDtypeStruct((m, d), BF16)] * n_out,
        compiler_params=_params(("arbitrary",), 32),
        name="rms_cast",
    )(x2, *[g.reshape(1, d) for g in gains])


def _ws_matmul(x, ws, extras, *, grid, x_spec, w_specs, extra_specs, out_shape, out_specs,
               epilogue, tn, recast, k_block=None, prefetch=(), valid=None, vmem_mib=48,
               name="ws_matmul"):
    n_w, n_e, n_pf = len(ws), len(extras), len(prefetch)
    outs = out_shape if isinstance(out_shape, (list, tuple)) else [out_shape]
    n_out = len(outs)
    k = x.shape[1] if k_block is None else k_block
    cast_rows = _tile(k, 512, 16)

    def body(*refs):
        pf, refs = refs[:n_pf], refs[n_pf:]
        x_ref = refs[0]
        w_refs = refs[1:1 + n_w]
        e_refs = refs[1 + n_w:1 + n_w + n_e]
        o_refs = refs[1 + n_w + n_e:1 + n_w + n_e + n_out]
        wb_ref = refs[1 + n_w + n_e + n_out]
        n, m = pl.program_id(0), pl.program_id(1)

        def compute():
            @pl.when(recast(pf, n, m))
            def _():
                def chunk(c, carry):
                    r0 = pl.multiple_of(c * cast_rows, cast_rows)
                    for j, w_ref in enumerate(w_refs):
                        wb_ref[pl.ds(r0, cast_rows), j * tn:(j + 1) * tn] = (
                            w_ref[pl.ds(r0, cast_rows), :].astype(BF16))
                    return carry
                lax.fori_loop(0, k // cast_rows, chunk, 0)

            acc = jnp.dot(x_ref[...], wb_ref[...], preferred_element_type=F32)
            epilogue(acc, e_refs, o_refs, pf, n, m)

        if valid is None:
            compute()
        else:
            ok = valid(pf, n, m)
            pl.when(ok)(compute)

            @pl.when(jnp.logical_not(ok))
            def _():
                for o_ref in o_refs:
                    o_ref[...] = jnp.zeros(o_ref.shape, o_ref.dtype)

    return pl.pallas_call(
        body,
        grid_spec=pltpu.PrefetchScalarGridSpec(
            num_scalar_prefetch=n_pf,
            grid=grid,
            in_specs=[x_spec] + list(w_specs) + list(extra_specs),
            out_specs=out_specs,
            scratch_shapes=[pltpu.VMEM((k, n_w * tn), BF16)],
        ),
        out_shape=out_shape,
        compiler_params=_params(("arbitrary", "arbitrary"), vmem_mib),
        name=name,
    )(*prefetch, x, *ws, *extras)


def _first_row_tile(pf, n, m):
    return m == 0


def _silu_mul(acc, tn):
    gate, up = acc[:, :tn], acc[:, tn:]
    return gate * jax.nn.sigmoid(gate) * up


def _gate_up(xn, w_gate_up):
    m, k = xn.shape
    dff = w_gate_up.shape[1] // 2
    tm, tn = _tile(m, 1024, 8), _tile(dff, 256, LANES)
    nt = dff // tn

    def epilogue(acc, e_refs, o_refs, pf, n, mi):
        o_refs[0][...] = _silu_mul(acc, tn).astype(BF16)

    return _ws_matmul(
        xn, [w_gate_up, w_gate_up], [],
        grid=(nt, m // tm),
        x_spec=pl.BlockSpec((tm, k), lambda n, mi: (mi, 0)),
        w_specs=[pl.BlockSpec((k, tn), lambda n, mi: (0, n)),
                 pl.BlockSpec((k, tn), lambda n, mi: (0, n + nt))],
        extra_specs=[],
        out_shape=jax.ShapeDtypeStruct((m, dff), BF16),
        out_specs=pl.BlockSpec((tm, tn), lambda n, mi: (mi, n)),
        epilogue=epilogue, tn=tn, recast=_first_row_tile, name="dense_gate_up")


MAX_WEIGHT_TILE_ROWS = 6144


def _matmul_residual(x, w, res, *, name):
    m, k = x.shape
    n_out = w.shape[1]
    n_chunks = pl.cdiv(k, MAX_WEIGHT_TILE_ROWS)
    assert k % (n_chunks * LANES) == 0
    kc = k // n_chunks
    tm, tn = _tile(m, 1024 if kc <= 4096 else 512, 8), _tile(n_out, 512, LANES)

    def epilogue(acc, e_refs, o_refs, pf, n, mi):
        o_refs[0][...] = e_refs[0][...] + acc

    for c in range(n_chunks):
        res = _ws_matmul(
            x, [w], [res],
            grid=(n_out // tn, m // tm),
            x_spec=pl.BlockSpec((tm, kc), lambda n, mi, c=c: (mi, c)),
            w_specs=[pl.BlockSpec((kc, tn), lambda n, mi, c=c: (c, n))],
            extra_specs=[pl.BlockSpec((tm, tn), lambda n, mi: (mi, n))],
            out_shape=jax.ShapeDtypeStruct((m, n_out), F32),
            out_specs=pl.BlockSpec((tm, tn), lambda n, mi: (mi, n)),
            epilogue=epilogue, tn=tn, k_block=kc, recast=_first_row_tile, vmem_mib=56,
            name=name)
    return res


def _ple(hn, w_gate, p_bf16, w_ple, h, layer):
    m, k = hn.shape
    d = w_gate.shape[2]
    pdim = p_bf16.shape[2]
    tm, tn = _tile(m, 1024, 8), _tile(d, 512, LANES)

    def epilogue(acc, e_refs, o_refs, pf, n, mi):
        p_ref, wp_ref, h_ref = e_refs
        emb = jnp.dot(p_ref[...], wp_ref[...].astype(BF16), preferred_element_type=F32)
        o_refs[0][...] = h_ref[...] + emb * jax.nn.sigmoid(acc)

    return _ws_matmul(
        hn, [w_gate], [p_bf16, w_ple, h],
        grid=(d // tn, m // tm),
        x_spec=pl.BlockSpec((tm, k), lambda n, mi: (mi, 0)),
        w_specs=[pl.BlockSpec((None, k, tn), lambda n, mi: (layer, 0, n))],
        extra_specs=[pl.BlockSpec((None, tm, pdim), lambda n, mi: (layer, mi, 0)),
                     pl.BlockSpec((None, pdim, tn), lambda n, mi: (layer, 0, n)),
                     pl.BlockSpec((tm, tn), lambda n, mi: (mi, n))],
        out_shape=jax.ShapeDtypeStruct((m, d), F32),
        out_specs=pl.BlockSpec((tm, tn), lambda n, mi: (mi, n)),
        epilogue=epilogue, tn=tn, recast=_first_row_tile, vmem_mib=56, name="ple")


def _rope_tables(seq):
    half = HEAD_DIM // 2
    inv_freq = jnp.exp(-math.log(ROPE_THETA) * jnp.arange(half, dtype=F32) / half)
    ang = jnp.arange(seq, dtype=F32)[:, None] * inv_freq[None, :]
    cos, sin = jnp.cos(ang), jnp.sin(ang)
    return jnp.concatenate([cos, cos], axis=-1), jnp.concatenate([-sin, sin], axis=-1)


def _norm_rope_heads(acc, gain, cos, sin_signed, out_ref, tn, post_scale):
    for hh in range(tn // HEAD_DIM):
        sl = slice(hh * HEAD_DIM, (hh + 1) * HEAD_DIM)
        y = acc[:, sl]
        y = y * _rms_scale(y) * gain
        y = y * cos + pltpu.roll(y, HEAD_DIM // 2, axis=1) * sin_signed
        if post_scale != 1.0:
            y = y * post_scale
        out_ref[:, sl] = y.astype(out_ref.dtype)


def _qk_projection(xn, w, head_gain, cos, sin_signed, seq, *, n_rope_cols, post_scale, name):
    m, k = xn.shape
    n_out = w.shape[1]
    tm, tn = _tile(seq, 1024, 8), _tile(min(n_out, n_rope_cols), 512, LANES)
    n_rope_tiles = n_rope_cols // tn
    spb = seq // tm

    def epilogue(acc, e_refs, o_refs, pf, n, mi):
        g_ref, c_ref, s_ref = e_refs

        @pl.when(n < n_rope_tiles)
        def _():
            _norm_rope_heads(acc, g_ref[...], c_ref[...], s_ref[...], o_refs[0], tn, post_scale)

        if n_rope_tiles * tn < n_out:
            @pl.when(n >= n_rope_tiles)
            def _():
                o_refs[0][...] = acc.astype(BF16)

    return _ws_matmul(
        xn, [w], [head_gain.reshape(1, HEAD_DIM), cos, sin_signed],
        grid=(n_out // tn, m // tm),
        x_spec=pl.BlockSpec((tm, k), lambda n, mi: (mi, 0)),
        w_specs=[pl.BlockSpec((k, tn), lambda n, mi: (0, n))],
        extra_specs=[pl.BlockSpec((1, HEAD_DIM), lambda n, mi: (0, 0)),
                     pl.BlockSpec((tm, HEAD_DIM), lambda n, mi: (mi % spb, 0)),
                     pl.BlockSpec((tm, HEAD_DIM), lambda n, mi: (mi % spb, 0))],
        out_shape=jax.ShapeDtypeStruct((m, n_out), BF16),
        out_specs=pl.BlockSpec((tm, tn), lambda n, mi: (mi, n)),
        epilogue=epilogue, tn=tn, recast=_first_row_tile, vmem_mib=56, name=name)


def _attn_body(q_ref, k_ref, v_ref, o_ref, qs_ref, ks_ref, vs_ref, acc_ref, m_ref, l_ref,
               s_ref, p_ref, a_ref, *, seq):
    blk = ATTN_BLOCK
    regroup = max(d for _, d in DILATED_BRANCHES)
    res_rows = seq // regroup
    row = lax.broadcasted_iota(jnp.int32, (blk, blk), 0)
    col = lax.broadcasted_iota(jnp.int32, (blk, blk), 1)
    contract_last = (((1,), (1,)), ((), ()))
    order = sorted(DILATED_BRANCHES, key=lambda wd: -wd[1])
    assert order[-1][1] == 1 and regroup % 8 == 0 and blk % regroup == 0

    def seq_pos(idx, pieces):
        n = blk // pieces
        return pieces * (idx % n) + idx // n

    def pieces_of(d, c, i):
        n = blk * d // regroup
        return [((c + d * a) * res_rows + n * i, n) for a in range(regroup // d)]

    def load(ref, pieces):
        parts = [ref[pl.ds(s, n), :] for s, n in pieces]
        return parts[0] if len(parts) == 1 else jnp.concatenate(parts, axis=0)

    def store(ref, pieces, val):
        off = 0
        for s, n in pieces:
            ref[pl.ds(s, n), :] = val[off:off + n]
            off += n

    gather_rows = (col == seq_pos(row, regroup)).astype(BF16)
    scatter_rows = (row == seq_pos(col, regroup)).astype(BF16)
    for i in range(seq // blk):
        for src, dst in ((q_ref, qs_ref), (k_ref, ks_ref), (v_ref, vs_ref)):
            regrouped = jnp.dot(gather_rows, src[pl.ds(i * blk, blk), :], preferred_element_type=F32)
            store(dst, pieces_of(1, 0, i), regrouped)

    for step, (window, d) in enumerate(order):
        first, last = step == 0, step == len(order) - 1
        pieces = regroup // d
        pos_r, pos_c = seq_pos(row, pieces), seq_pos(col, pieces)
        bias_cur = jnp.where(pos_c <= pos_r, 0.0, NEG).astype(F32)
        bias_prev = jnp.where(pos_c >= pos_r, 0.0, NEG).astype(F32)
        tiles = [(pieces_of(d, c, i), pieces_of(d, c, i - 1) if i > 0 else None, i)
                 for c in range(d) for i in range(seq // d // blk)]

        for t, (cur, prev, i) in enumerate(tiles):
            q = load(qs_ref, cur).astype(BF16)
            s_ref[t, :, :blk] = lax.dot_general(q, load(ks_ref, cur).astype(BF16), contract_last,
                                                preferred_element_type=F32) + bias_cur
            if prev is not None:
                s_ref[t, :, blk:] = lax.dot_general(q, load(ks_ref, prev).astype(BF16), contract_last,
                                                    preferred_element_type=F32) + bias_prev

        for t, (cur, prev, i) in enumerate(tiles):
            s_c = s_ref[t, :, :blk]
            mx = jnp.max(s_c, axis=-1, keepdims=True)
            if prev is not None:
                s_p = s_ref[t, :, blk:]
                mx = jnp.maximum(mx, jnp.max(s_p, axis=-1, keepdims=True))
            if first:
                m_new = jnp.broadcast_to(mx, (blk, LANES))
            else:
                m_old = load(m_ref, cur)
                m_new = jnp.maximum(m_old, mx)
            p_c = jnp.exp(s_c - m_new)
            den = jnp.sum(p_c, axis=-1, keepdims=True)
            p_ref[t, :, :blk] = p_c.astype(BF16)
            if prev is not None:
                p_p = jnp.exp(s_p - m_new)
                den = den + jnp.sum(p_p, axis=-1, keepdims=True)
                p_ref[t, :, blk:] = p_p.astype(BF16)
            if first:
                den = jnp.broadcast_to(den, (blk, LANES))
            else:
                alpha = jnp.exp(m_old - m_new)
                a_ref[t] = alpha
                den = alpha * load(l_ref, cur) + den
            store(l_ref, cur, den)
            if not last:
                store(m_ref, cur, m_new)

        for t, (cur, prev, i) in enumerate(tiles):
            acc = jnp.dot(p_ref[t, :, :blk], load(vs_ref, cur).astype(BF16), preferred_element_type=F32)
            if prev is not None:
                acc = acc + jnp.dot(p_ref[t, :, blk:], load(vs_ref, prev).astype(BF16),
                                    preferred_element_type=F32)
            if not first:
                acc = a_ref[t] * load(acc_ref, cur) + acc
            if last:
                out = jnp.dot(scatter_rows, (acc / load(l_ref, cur)).astype(BF16),
                              preferred_element_type=F32)
                o_ref[pl.ds(i * blk, blk), :] = out.astype(o_ref.dtype)
            else:
                store(acc_ref, cur, acc)


def _dilated_attention(q, kv, *, batch, seq):
    m, width = q.shape
    assert ATTN_BLOCK == LANES
    for window, d in DILATED_BRANCHES:
        assert window // d == ATTN_BLOCK and seq % (d * ATTN_BLOCK) == 0
    n_heads = width // HEAD_DIM
    head_block = pl.BlockSpec((seq, HEAD_DIM), lambda b, h: (b, h))
    return pl.pallas_call(
        functools.partial(_attn_body, seq=seq),
        grid=(batch, n_heads),
        in_specs=[head_block, head_block, pl.BlockSpec((seq, HEAD_DIM), lambda b, h: (b, n_heads + h))],
        out_specs=head_block,
        out_shape=jax.ShapeDtypeStruct((m, width), BF16),
        scratch_shapes=[pltpu.VMEM((seq, LANES), F32)] * 6 + [
            pltpu.VMEM((seq // ATTN_BLOCK, ATTN_BLOCK, 2 * ATTN_BLOCK), F32),
            pltpu.VMEM((seq // ATTN_BLOCK, ATTN_BLOCK, 2 * ATTN_BLOCK), BF16),
            pltpu.VMEM((seq // ATTN_BLOCK, ATTN_BLOCK, LANES), F32)],

        compiler_params=_params(("arbitrary", "arbitrary"), 32),
        name="dilated_attention",
    )(q, kv, kv)


def _router_body(h_ref, gain_ref, r_ref, idx_ref, gate_ref):
    x = h_ref[...]
    hn = x * _rms_scale(x) * gain_ref[...]
    logits = jnp.dot(hn, r_ref[...], preferred_element_type=F32, precision=lax.Precision.HIGHEST)
    lane = lax.broadcasted_iota(jnp.int32, logits.shape, 1).astype(F32)
    logits = jnp.where(lane < N_EXPERTS, logits, NEG)
    m1 = jnp.max(logits, axis=-1, keepdims=True)
    i1 = jnp.min(jnp.where(logits == m1, lane, float(LANES)), axis=-1, keepdims=True)
    rest = jnp.where(lane == i1, NEG, logits)
    m2 = jnp.max(rest, axis=-1, keepdims=True)
    i2 = jnp.min(jnp.where(rest == m2, lane, float(LANES)), axis=-1, keepdims=True)
    e2 = jnp.exp(m2 - m1)
    g1 = 1.0 / (1.0 + e2)
    g2 = e2 / (1.0 + e2)
    idx_ref[...] = jnp.where(lane == 0, i1, jnp.where(lane == 1, i2, 0.0)).astype(jnp.int32)
    gate_ref[...] = jnp.where(lane == 0, g1, jnp.where(lane == 1, g2, 0.0))


def _router(h, gain, router):
    m, d = h.shape
    n_e = router.shape[1]
    assert n_e == N_EXPERTS
    tm = _tile(m, 256, 8)
    r_pad = jnp.zeros((d, LANES), F32).at[:, :n_e].set(router)
    row = pl.BlockSpec((tm, d), lambda i: (i, 0))
    small = pl.BlockSpec((tm, LANES), lambda i: (i, 0))
    return pl.pallas_call(
        _router_body,
        grid=(m // tm,),
        in_specs=[row, pl.BlockSpec((1, d), lambda i: (0, 0)), pl.BlockSpec((d, LANES), lambda i: (0, 0))],
        out_specs=[small, small],
        out_shape=[jax.ShapeDtypeStruct((m, LANES), jnp.int32), jax.ShapeDtypeStruct((m, LANES), F32)],
        compiler_params=_params(("arbitrary",), 32),
        name="router_top2",
    )(h, gain.reshape(1, d), r_pad)


def _dispatch_plan(top_idx, tm):
    m = top_idx.shape[0]
    n_tiles = (m * TOP_K + N_EXPERTS * (tm - 1)) // tm
    flat_e = top_idx.reshape(-1)
    onehot = (flat_e[:, None] == jnp.arange(N_EXPERTS)[None, :]).astype(jnp.int32)
    rank = jnp.sum((jnp.cumsum(onehot, axis=0) - 1) * onehot, axis=1)
    counts = jnp.sum(onehot, axis=0)
    tiles_per = (counts + tm - 1) // tm
    tile_end = jnp.cumsum(tiles_per)
    row_start = (tile_end - tiles_per) * tm
    dest = row_start[flat_e] + rank
    src = jnp.zeros((n_tiles * tm,), jnp.int32).at[dest].set(jnp.arange(m * TOP_K, dtype=jnp.int32) // TOP_K)
    n_used = tile_end[-1]
    t = jnp.minimum(jnp.arange(n_tiles, dtype=jnp.int32), n_used - 1)
    tile_expert = jnp.sum((t[:, None] >= tile_end[None, :]).astype(jnp.int32), axis=1)
    tile_first = (t == (tile_end - tiles_per)[tile_expert]).astype(jnp.int32)
    tile_valid = (jnp.arange(n_tiles) < n_used).astype(jnp.int32)
    return (src, dest.reshape(m, TOP_K).astype(jnp.int32), tile_expert.astype(jnp.int32), t,
            tile_first, tile_valid, (n_used * tm).astype(jnp.int32).reshape(1))


DMA_LOOP_UNROLL = 8


def _row_dma_loop(n, copy_of, action):
    def step(j, c):
        getattr(copy_of(j), action)()
        return c
    lax.fori_loop(0, n, step, 0, unroll=DMA_LOOP_UNROLL)


DMA_CHUNKS = 8


def _pipelined_row_tiles(i, cur_used, next_used, copy, n_copies, consume, skip):
    per = n_copies // DMA_CHUNKS

    @pl.when((i == 0) & cur_used)
    def _():
        _row_dma_loop(n_copies, lambda q: copy(0, 0, q), "start")

    if skip is not None:
        pl.when(jnp.logical_not(cur_used))(skip)

    for s in range(2):
        for prefetch in (True, False):
            @pl.when(cur_used & (i % 2 == s) & (next_used if prefetch else jnp.logical_not(next_used)))
            def _():
                for c in range(DMA_CHUNKS):
                    if prefetch:
                        for q in range(c * per, (c + 1) * per):
                            copy(i + 1, 1 - s, q).start()
                    for q in range(c * per, (c + 1) * per):
                        copy(i, s, q).wait()
                    consume(s, c)


def _gather_body(src_ref, nrows_ref, gain_ref, h_hbm, o_ref, buf0_ref, buf1_ref, sems, *, rows):
    i = pl.program_id(0)
    bufs = (buf0_ref, buf1_ref)
    per = rows // DMA_CHUNKS

    def tile_used(t):
        return t * rows < nrows_ref[0]

    def copy(t, s, q):
        return pltpu.make_async_copy(h_hbm.at[pl.ds(src_ref[t * rows + q], 1)], bufs[s].at[pl.ds(q, 1)],
                                     sems.at[s, q // per])

    def consume(s, c):
        r = pl.ds(c * per, per)
        x = bufs[s][r, :]
        o_ref[r, :] = (x * _rms_scale(x) * gain_ref[...]).astype(BF16)

    def skip():
        o_ref[...] = jnp.zeros(o_ref.shape, o_ref.dtype)

    _pipelined_row_tiles(i, tile_used(i), (i + 1 < pl.num_programs(0)) & tile_used(i + 1), copy, rows,
                         consume, skip)


def _gather_norm_rows(h, gain, src, n_rows_used, rows):
    p = src.shape[0]
    d = h.shape[1]
    return pl.pallas_call(
        functools.partial(_gather_body, rows=rows),
        grid_spec=pltpu.PrefetchScalarGridSpec(
            num_scalar_prefetch=2,
            grid=(p // rows,),
            in_specs=[pl.BlockSpec((1, d), lambda i, src, nr: (0, 0)), pl.BlockSpec(memory_space=pl.ANY)],
            out_specs=pl.BlockSpec((rows, d), lambda i, src, nr: (i, 0)),
            scratch_shapes=[pltpu.VMEM((rows, d), F32), pltpu.VMEM((rows, d), F32),
                            pltpu.SemaphoreType.DMA((2, DMA_CHUNKS))],
        ),
        out_shape=jax.ShapeDtypeStruct((p, d), BF16),
        compiler_params=_params(("arbitrary",), 40),
        name="moe_gather_rows",
    )(src, n_rows_used, gain.reshape(1, d), h)


def _expert_changed(pf, n, m):
    return pf[2][m] == 1


def _tile_valid(pf, n, m):
    return pf[3][m] == 1


def _moe_gate_up(xs, w_gate_up, plan, tm):
    p, k = xs.shape
    dff = w_gate_up.shape[2] // 2
    tn = _tile(dff, 512, LANES)
    nt = dff // tn

    def epilogue(acc, e_refs, o_refs, pf, n, mi):
        o_refs[0][...] = _silu_mul(acc, tn).astype(BF16)

    return _ws_matmul(
        xs, [w_gate_up, w_gate_up], [],
        grid=(nt, p // tm),
        x_spec=pl.BlockSpec((tm, k), lambda n, mi, te, tr, tf, tv: (tr[mi], 0)),
        w_specs=[pl.BlockSpec((None, k, tn), lambda n, mi, te, tr, tf, tv: (te[mi], 0, n)),
                 pl.BlockSpec((None, k, tn), lambda n, mi, te, tr, tf, tv: (te[mi], 0, n + nt))],
        extra_specs=[],
        out_shape=jax.ShapeDtypeStruct((p, dff), BF16),
        out_specs=pl.BlockSpec((tm, tn), lambda n, mi, te, tr, tf, tv: (mi, n)),
        epilogue=epilogue, tn=tn, prefetch=plan, recast=_expert_changed, valid=_tile_valid,
        vmem_mib=56, name="moe_gate_up")


def _moe_down(hs, w_down, plan, tm):
    p, k = hs.shape
    d = w_down.shape[2]
    tn = _tile(d, 1024, LANES)

    def epilogue(acc, e_refs, o_refs, pf, n, mi):
        o_refs[0][...] = acc

    return _ws_matmul(
        hs, [w_down], [],
        grid=(d // tn, p // tm),
        x_spec=pl.BlockSpec((tm, k), lambda n, mi, te, tr, tf, tv: (tr[mi], 0)),
        w_specs=[pl.BlockSpec((None, k, tn), lambda n, mi, te, tr, tf, tv: (te[mi], 0, n))],
        extra_specs=[],
        out_shape=jax.ShapeDtypeStruct((p, d), F32),
        out_specs=pl.BlockSpec((tm, tn), lambda n, mi, te, tr, tf, tv: (mi, n)),
        epilogue=epilogue, tn=tn, prefetch=plan, recast=_expert_changed, valid=_tile_valid,
        vmem_mib=56, name="moe_down")


def _combine_body(pos_ref, h_ref, gate_ref, gain_ref, ys_hbm, h_out_ref, hn_ref, buf0_ref, buf1_ref, sems,
                  *, tm):
    i = pl.program_id(0)
    bufs = (buf0_ref, buf1_ref)
    n_copies = tm * TOP_K
    per = n_copies // DMA_CHUNKS
    tokens_per = per // TOP_K

    def copy(t, s, q):
        return pltpu.make_async_copy(ys_hbm.at[pl.ds(pos_ref[t * n_copies + q], 1)],
                                     bufs[s].at[q % TOP_K, pl.ds(q // TOP_K, 1)], sems.at[s, q // per])

    def consume(s, c):
        r = pl.ds(c * tokens_per, tokens_per)
        h = h_ref[r, :]
        gates = gate_ref[r, :]
        for k in range(TOP_K):
            h = h + gates[:, k:k + 1] * bufs[s][k, r, :]
        h_out_ref[r, :] = h
        hn_ref[r, :] = (h * _rms_scale(h) * gain_ref[...]).astype(BF16)

    _pipelined_row_tiles(i, i >= 0, i + 1 < pl.num_programs(0), copy, n_copies, consume, None)


def _moe_combine(h, ys, pos, gates, next_gain):
    m, d = h.shape
    tm = _tile(m, 256, 8)
    row = pl.BlockSpec((tm, d), lambda i, pos: (i, 0))
    return pl.pallas_call(
        functools.partial(_combine_body, tm=tm),
        grid_spec=pltpu.PrefetchScalarGridSpec(
            num_scalar_prefetch=1,
            grid=(m // tm,),
            in_specs=[row, pl.BlockSpec((tm, LANES), lambda i, pos: (i, 0)),
                      pl.BlockSpec((1, d), lambda i, pos: (0, 0)), pl.BlockSpec(memory_space=pl.ANY)],
            out_specs=[row, row],
            scratch_shapes=[pltpu.VMEM((TOP_K, tm, d), F32), pltpu.VMEM((TOP_K, tm, d), F32),
                            pltpu.SemaphoreType.DMA((2, DMA_CHUNKS))],
        ),
        out_shape=[jax.ShapeDtypeStruct((m, d), F32), jax.ShapeDtypeStruct((m, d), BF16)],
        compiler_params=_params(("arbitrary",), 48),
        name="moe_combine",
    )(pos.reshape(-1), h, gates, next_gain.reshape(1, d), ys)


def kernel(x, p, pool_norm, pool_w, pool_scale, kv_norm, w_kv, k_norm, attn_norm, w_q, q_norm, w_o,
           ffn_norm, dense_w_gate_up, dense_w_down, moe_router, moe_w_gate_up, moe_w_down, ple_w,
           ple_norm, ple_gate_w):
    batch, seq, d = x.shape
    m = batch * seq
    p_bf16 = p.astype(BF16).reshape(p.shape[0], m, p.shape[-1])
    cos, sin_signed = _rope_tables(seq)

    h, hn = _pool_layer(x.reshape(m, d), seq, pool_norm[0], pool_w[0], pool_scale[0], ffn_norm[0])
    act = _gate_up(hn, dense_w_gate_up[0])
    h = _matmul_residual(act, dense_w_down[0], h, name="dense_down")
    (hn,) = _rms_cast(h, [ple_norm[0]])
    h = _ple(hn, ple_gate_w, p_bf16, ple_w, h, 0)

    hn_kv, hn_q = _rms_cast(h, [kv_norm, attn_norm[0]])
    width = w_q.shape[2]
    kv = _qk_projection(hn_kv, w_kv, k_norm, cos, sin_signed, seq, n_rope_cols=width,
                        post_scale=1.0, name="kv_proj")
    q = _qk_projection(hn_q, w_q[0], q_norm[0], cos, sin_signed, seq, n_rope_cols=width,
                       post_scale=1.0 / math.sqrt(HEAD_DIM), name="q_proj")

    attn = _dilated_attention(q, kv, batch=batch, seq=seq)
    h = _matmul_residual(attn, w_o[0], h, name="attn_out_proj")

    top_idx, gates = _router(h, ffn_norm[1], moe_router[0])
    tm_moe = _tile(m, 512, 8)
    src, pos, tile_expert, tile_row, tile_first, tile_valid, n_rows_used = _dispatch_plan(
        top_idx[:, :TOP_K], tm_moe)
    plan = (tile_expert, tile_row, tile_first, tile_valid)
    xs = _gather_norm_rows(h, ffn_norm[1], src, n_rows_used, tm_moe)
    hs = _moe_gate_up(xs, moe_w_gate_up[0], plan, tm_moe)
    ys = _moe_down(hs, moe_w_down[0], plan, tm_moe)
    h, hn = _moe_combine(h, ys, pos, gates, ple_norm[1])
    h = _ple(hn, ple_gate_w, p_bf16, ple_w, h, 1)
    return h.reshape(batch, seq, d)
```

```python
import functools
import math

import jax
import jax.numpy as jnp
from jax import lax
from jax.experimental import pallas as pl
from jax.experimental.pallas import tpu as pltpu

F32 = jnp.float32
BF16 = jnp.bfloat16

HEAD_DIM = 128
LANES = 128
ROPE_THETA = 10000.0
RMS_EPS = 1e-6
POOL_WINDOWS = (2, 4, 8, 16)
POOL_HALO = 32
DILATED_BRANCHES = ((128, 1), (512, 4), (2048, 16))
ATTN_BLOCK = 128
N_EXPERTS = 8
TOP_K = 2
NEG = -1e30
ROW_QUANTUM = 128
MIB = 1024 * 1024


def _params(semantics, vmem_mib):
    return pltpu.CompilerParams(dimension_semantics=semantics, vmem_limit_bytes=vmem_mib * MIB)


def _tile(dim, pref, quantum):
    if dim <= pref:
        return dim
    t = (pref // quantum) * quantum
    while t > quantum and dim % t:
        t -= quantum
    assert dim % t == 0, (dim, pref, quantum)
    return t


def _rms_scale(x):
    return lax.rsqrt(jnp.mean(x * x, axis=-1, keepdims=True) + RMS_EPS)


def _pool_body(x_ref, halo_ref, w_ref, gain_ref, scale_ref, next_gain_ref,
               h_ref, hn_ref, yn_ref, hrow_ref, lvl_ref, *, ts, seq, n_groups, gdim):
    i = pl.program_id(0)
    g = pl.program_id(1)
    t0 = (i * ts) % seq

    @pl.when(g == 0)
    def _():
        x = x_ref[...]
        y = x * _rms_scale(x) * gain_ref[...]
        xh = halo_ref[...]
        yh = xh * _rms_scale(xh) * gain_ref[...]
        yh = yh * (t0 > 0).astype(F32)
        for gi in range(n_groups):
            yn_ref[gi, :POOL_HALO, :] = yh[:, gi * gdim:(gi + 1) * gdim]
            yn_ref[gi, POOL_HALO:, :] = y[:, gi * gdim:(gi + 1) * gdim]

    pos = t0 + lax.broadcasted_iota(jnp.int32, (ts, 1), 0)
    for gi, win in enumerate(POOL_WINDOWS):
        @pl.when(g == gi)
        def _():
            levels = win.bit_length() - 1
            assert win == 1 << levels and 8 * levels <= POOL_HALO
            src = yn_ref.at[gi]
            for lv in range(1, levels + 1):
                shift, lo = 1 << (lv - 1), 8 * lv
                dst = lvl_ref.at[lv % 2]
                dst[lo:, :] = src[lo:, :] + src[lo - shift:ts + POOL_HALO - shift, :]
                src = dst
            cur = yn_ref[gi, POOL_HALO:, :]
            acc = src[POOL_HALO:, :]
            inv_cnt = 1.0 / jnp.minimum(pos + 1, win).astype(F32)
            pooled = (acc * inv_cnt - cur).astype(BF16)
            sl = slice(gi * gdim, (gi + 1) * gdim)
            mixed = jnp.dot(pooled, w_ref[gi], preferred_element_type=F32)
            hrow_ref[gi] = x_ref[:, sl] + mixed * scale_ref[:, sl]

    h_ref[...] = hrow_ref[g]

    @pl.when(g == n_groups - 1)
    def _():
        ss = jnp.zeros((ts, 1), F32)
        for gi in range(n_groups):
            hg = hrow_ref[gi]
            ss = ss + jnp.sum(hg * hg, axis=-1, keepdims=True)
        inv = lax.rsqrt(ss / (n_groups * gdim) + RMS_EPS)
        for gi in range(n_groups):
            sl = slice(gi * gdim, (gi + 1) * gdim)
            hn_ref[:, sl] = (hrow_ref[gi] * inv * next_gain_ref[:, sl]).astype(BF16)


def _pool_layer(x2, seq, norm_gain, w_groups, scale, next_gain):
    m, d = x2.shape
    n_groups, gdim = w_groups.shape[0], w_groups.shape[1]
    assert n_groups == len(POOL_WINDOWS) and n_groups * gdim == d
    ts = _tile(seq, 256, POOL_HALO)
    hpt = ts // POOL_HALO
    body = functools.partial(_pool_body, ts=ts, seq=seq, n_groups=n_groups, gdim=gdim)
    return pl.pallas_call(
        body,
        grid=(m // ts, n_groups),
        in_specs=[
            pl.BlockSpec((ts, d), lambda i, g: (i, 0)),
            pl.BlockSpec((POOL_HALO, d), lambda i, g: (jnp.maximum(i * hpt - 1, 0), 0)),
            pl.BlockSpec((n_groups, gdim, gdim), lambda i, g: (0, 0, 0)),
            pl.BlockSpec((1, d), lambda i, g: (0, 0)),
            pl.BlockSpec((1, d), lambda i, g: (0, 0)),
            pl.BlockSpec((1, d), lambda i, g: (0, 0)),
        ],
        out_specs=[
            pl.BlockSpec((ts, gdim), lambda i, g: (i, g)),
            pl.BlockSpec((ts, d), lambda i, g: (i, 0)),
        ],
        out_shape=[jax.ShapeDtypeStruct((m, d), F32), jax.ShapeDtypeStruct((m, d), BF16)],
        scratch_shapes=[
            pltpu.VMEM((n_groups, ts + POOL_HALO, gdim), F32),
            pltpu.VMEM((n_groups, ts, gdim), F32),
            pltpu.VMEM((2, ts + POOL_HALO, gdim), F32),
        ],
        compiler_params=_params(("arbitrary", "arbitrary"), 48),
        name="pool_layer",
    )(x2, x2, w_groups.astype(BF16), norm_gain.reshape(1, d), scale.reshape(1, d), next_gain.reshape(1, d))


def _rms_body(x_ref, *refs, n_out):
    gains, outs = refs[:n_out], refs[n_out:]
    x = x_ref[...]
    y = x * _rms_scale(x)
    for g_ref, o_ref in zip(gains, outs):
        o_ref[...] = (y * g_ref[...]).astype(BF16)


def _rms_cast(x2, gains):
    m, d = x2.shape
    tm = _tile(m, 256, 8)
    n_out = len(gains)
    row = pl.BlockSpec((tm, d), lambda i: (i, 0))
    vec = pl.BlockSpec((1, d), lambda i: (0, 0))
    return pl.pallas_call(
        functools.partial(_rms_body, n_out=n_out),
        grid=(m // tm,),
        in_specs=[row] + [vec] * n_out,
        out_specs=[row] * n_out,
        out_shape=[jax.ShapeDtypeStruct((m, d), BF16)] * n_out,
        compiler_params=_params(("arbitrary",), 32),
        name="rms_cast",
    )(x2, *[g.reshape(1, d) for g in gains])


def _ws_matmul(x, ws, extras, *, grid, x_spec, w_specs, extra_specs, out_shape, out_specs,
               epilogue, tn, recast, k_block=None, prefetch=(), live_rows=None, vmem_mib=48,
               name="ws_matmul"):
    n_w, n_e, n_pf = len(ws), len(extras), len(prefetch)
    outs = out_shape if isinstance(out_shape, (list, tuple)) else [out_shape]
    n_out = len(outs)
    k = x.shape[1] if k_block is None else k_block
    cast_rows = _tile(k, 512, 16)

    def body(*refs):
        pf, refs = refs[:n_pf], refs[n_pf:]
        x_ref = refs[0]
        w_refs = refs[1:1 + n_w]
        e_refs = refs[1 + n_w:1 + n_w + n_e]
        o_refs = refs[1 + n_w + n_e:1 + n_w + n_e + n_out]
        wb_ref = refs[1 + n_w + n_e + n_out]
        n, m = pl.program_id(0), pl.program_id(1)

        def cast_weights():
            def chunk(c, carry):
                r0 = pl.multiple_of(c * cast_rows, cast_rows)
                for j, w_ref in enumerate(w_refs):
                    wb_ref[pl.ds(r0, cast_rows), j * tn:(j + 1) * tn] = (
                        w_ref[pl.ds(r0, cast_rows), :].astype(BF16))
                return carry
            lax.fori_loop(0, k // cast_rows, chunk, 0)

        if live_rows is None:
            pl.when(recast(pf, n, m))(cast_weights)
            epilogue(jnp.dot(x_ref[...], wb_ref[...], preferred_element_type=F32), e_refs, o_refs, pf, n, m,
                     None)
        else:
            tm = x_ref.shape[0]
            live = live_rows(pf, n, m)
            pl.when((live > 0) & recast(pf, n, m))(cast_weights)
            for rows in range(0, tm + 1, ROW_QUANTUM):
                @pl.when(live == rows)
                def _():
                    if rows:
                        acc = jnp.dot(x_ref[:rows, :], wb_ref[...], preferred_element_type=F32)
                        epilogue(acc, e_refs, o_refs, pf, n, m, rows)
                    if rows < tm:
                        for o_ref in o_refs:
                            o_ref[rows:, :] = jnp.zeros((tm - rows, o_ref.shape[1]), o_ref.dtype)

    return pl.pallas_call(
        body,
        grid_spec=pltpu.PrefetchScalarGridSpec(
            num_scalar_prefetch=n_pf,
            grid=grid,
            in_specs=[x_spec] + list(w_specs) + list(extra_specs),
            out_specs=out_specs,
            scratch_shapes=[pltpu.VMEM((k, n_w * tn), BF16)],
        ),
        out_shape=out_shape,
        compiler_params=_params(("arbitrary", "arbitrary"), vmem_mib),
        name=name,
    )(*prefetch, x, *ws, *extras)


def _first_row_tile(pf, n, m):
    return m == 0


def _silu_mul(acc, tn):
    gate, up = acc[:, :tn], acc[:, tn:]
    return gate * jax.nn.sigmoid(gate) * up


def _gate_up(xn, w_gate_up):
    m, k = xn.shape
    dff = w_gate_up.shape[1] // 2
    tm, tn = _tile(m, 1024, 8), _tile(dff, 256, LANES)
    nt = dff // tn

    def epilogue(acc, e_refs, o_refs, pf, n, mi, rows):
        o_refs[0][...] = _silu_mul(acc, tn).astype(BF16)

    return _ws_matmul(
        xn, [w_gate_up, w_gate_up], [],
        grid=(nt, m // tm),
        x_spec=pl.BlockSpec((tm, k), lambda n, mi: (mi, 0)),
        w_specs=[pl.BlockSpec((k, tn), lambda n, mi: (0, n)),
                 pl.BlockSpec((k, tn), lambda n, mi: (0, n + nt))],
        extra_specs=[],
        out_shape=jax.ShapeDtypeStruct((m, dff), BF16),
        out_specs=pl.BlockSpec((tm, tn), lambda n, mi: (mi, n)),
        epilogue=epilogue, tn=tn, recast=_first_row_tile, name="dense_gate_up")


MAX_WEIGHT_TILE_ROWS = 6144


def _matmul_residual(x, w, res, *, name):
    m, k = x.shape
    n_out = w.shape[1]
    n_chunks = pl.cdiv(k, MAX_WEIGHT_TILE_ROWS)
    assert k % (n_chunks * LANES) == 0
    kc = k // n_chunks
    tm, tn = _tile(m, 1024 if kc <= 4096 else 512, 8), _tile(n_out, 512, LANES)

    def epilogue(acc, e_refs, o_refs, pf, n, mi, rows):
        o_refs[0][...] = e_refs[0][...] + acc

    for c in range(n_chunks):
        res = _ws_matmul(
            x, [w], [res],
            grid=(n_out // tn, m // tm),
            x_spec=pl.BlockSpec((tm, kc), lambda n, mi, c=c: (mi, c)),
            w_specs=[pl.BlockSpec((kc, tn), lambda n, mi, c=c: (c, n))],
            extra_specs=[pl.BlockSpec((tm, tn), lambda n, mi: (mi, n))],
            out_shape=jax.ShapeDtypeStruct((m, n_out), F32),
            out_specs=pl.BlockSpec((tm, tn), lambda n, mi: (mi, n)),
            epilogue=epilogue, tn=tn, k_block=kc, recast=_first_row_tile, vmem_mib=56,
            name=name)
    return res


def _ple(hn, w_gate, p_bf16, w_ple, h, layer):
    m, k = hn.shape
    d = w_gate.shape[2]
    pdim = p_bf16.shape[2]
    tm, tn = _tile(m, 1024, 8), _tile(d, 512, LANES)

    def epilogue(acc, e_refs, o_refs, pf, n, mi, rows):
        p_ref, wp_ref, h_ref = e_refs
        emb = jnp.dot(p_ref[...], wp_ref[...].astype(BF16), preferred_element_type=F32)
        o_refs[0][...] = h_ref[...] + emb * jax.nn.sigmoid(acc)

    return _ws_matmul(
        hn, [w_gate], [p_bf16, w_ple, h],
        grid=(d // tn, m // tm),
        x_spec=pl.BlockSpec((tm, k), lambda n, mi: (mi, 0)),
        w_specs=[pl.BlockSpec((None, k, tn), lambda n, mi: (layer, 0, n))],
        extra_specs=[pl.BlockSpec((None, tm, pdim), lambda n, mi: (layer, mi, 0)),
                     pl.BlockSpec((None, pdim, tn), lambda n, mi: (layer, 0, n)),
                     pl.BlockSpec((tm, tn), lambda n, mi: (mi, n))],
        out_shape=jax.ShapeDtypeStruct((m, d), F32),
        out_specs=pl.BlockSpec((tm, tn), lambda n, mi: (mi, n)),
        epilogue=epilogue, tn=tn, recast=_first_row_tile, vmem_mib=56, name="ple")


def _rope_tables(seq):
    half = HEAD_DIM // 2
    inv_freq = jnp.exp(-math.log(ROPE_THETA) * jnp.arange(half, dtype=F32) / half)
    ang = jnp.arange(seq, dtype=F32)[:, None] * inv_freq[None, :]
    cos, sin = jnp.cos(ang), jnp.sin(ang)
    return jnp.concatenate([cos, cos], axis=-1), jnp.concatenate([-sin, sin], axis=-1)


def _norm_rope_heads(acc, gain, cos, sin_signed, out_ref, tn, post_scale):
    for hh in range(tn // HEAD_DIM):
        sl = slice(hh * HEAD_DIM, (hh + 1) * HEAD_DIM)
        y = acc[:, sl]
        y = y * _rms_scale(y) * gain
        y = y * cos + pltpu.roll(y, HEAD_DIM // 2, axis=1) * sin_signed
        if post_scale != 1.0:
            y = y * post_scale
        out_ref[:, sl] = y.astype(out_ref.dtype)


def _qk_projection(xn, w, head_gain, cos, sin_signed, seq, *, n_rope_cols, post_scale, name):
    m, k = xn.shape
    n_out = w.shape[1]
    tm, tn = _tile(seq, 1024, 8), _tile(min(n_out, n_rope_cols), 512, LANES)
    n_rope_tiles = n_rope_cols // tn
    spb = seq // tm

    def epilogue(acc, e_refs, o_refs, pf, n, mi, rows):
        g_ref, c_ref, s_ref = e_refs

        @pl.when(n < n_rope_tiles)
        def _():
            _norm_rope_heads(acc, g_ref[...], c_ref[...], s_ref[...], o_refs[0], tn, post_scale)

        if n_rope_tiles * tn < n_out:
            @pl.when(n >= n_rope_tiles)
            def _():
                o_refs[0][...] = acc.astype(BF16)

    return _ws_matmul(
        xn, [w], [head_gain.reshape(1, HEAD_DIM), cos, sin_signed],
        grid=(n_out // tn, m // tm),
        x_spec=pl.BlockSpec((tm, k), lambda n, mi: (mi, 0)),
        w_specs=[pl.BlockSpec((k, tn), lambda n, mi: (0, n))],
        extra_specs=[pl.BlockSpec((1, HEAD_DIM), lambda n, mi: (0, 0)),
                     pl.BlockSpec((tm, HEAD_DIM), lambda n, mi: (mi % spb, 0)),
                     pl.BlockSpec((tm, HEAD_DIM), lambda n, mi: (mi % spb, 0))],
        out_shape=jax.ShapeDtypeStruct((m, n_out), BF16),
        out_specs=pl.BlockSpec((tm, tn), lambda n, mi: (mi, n)),
        epilogue=epilogue, tn=tn, recast=_first_row_tile, vmem_mib=56, name=name)


def _attn_body(q_ref, k_ref, v_ref, o_ref, qs_ref, ks_ref, vs_ref, acc_ref, m_ref, l_ref,
               s_ref, p_ref, a_ref, *, seq):
    blk = ATTN_BLOCK
    regroup = max(d for _, d in DILATED_BRANCHES)
    res_rows = seq // regroup
    row = lax.broadcasted_iota(jnp.int32, (blk, blk), 0)
    col = lax.broadcasted_iota(jnp.int32, (blk, blk), 1)
    contract_last = (((1,), (1,)), ((), ()))
    order = sorted(DILATED_BRANCHES, key=lambda wd: -wd[1])
    assert order[-1][1] == 1 and regroup % 8 == 0 and blk % regroup == 0

    def seq_pos(idx, pieces):
        n = blk // pieces
        return pieces * (idx % n) + idx // n

    def pieces_of(d, c, i):
        n = blk * d // regroup
        return [((c + d * a) * res_rows + n * i, n) for a in range(regroup // d)]

    def load(ref, pieces):
        parts = [ref[pl.ds(s, n), :] for s, n in pieces]
        return parts[0] if len(parts) == 1 else jnp.concatenate(parts, axis=0)

    def store(ref, pieces, val):
        off = 0
        for s, n in pieces:
            ref[pl.ds(s, n), :] = val[off:off + n]
            off += n

    gather_rows = (col == seq_pos(row, regroup)).astype(BF16)
    scatter_rows = (row == seq_pos(col, regroup)).astype(BF16)
    for i in range(seq // blk):
        for src, dst in ((q_ref, qs_ref), (k_ref, ks_ref), (v_ref, vs_ref)):
            regrouped = jnp.dot(gather_rows, src[pl.ds(i * blk, blk), :], preferred_element_type=F32)
            store(dst, pieces_of(1, 0, i), regrouped)

    for step, (window, d) in enumerate(order):
        first, last = step == 0, step == len(order) - 1
        pieces = regroup // d
        pos_r, pos_c = seq_pos(row, pieces), seq_pos(col, pieces)
        bias_cur = jnp.where(pos_c <= pos_r, 0.0, NEG).astype(F32)
        bias_prev = jnp.where(pos_c >= pos_r, 0.0, NEG).astype(F32)
        tiles = [(pieces_of(d, c, i), pieces_of(d, c, i - 1) if i > 0 else None, i)
                 for c in range(d) for i in range(seq // d // blk)]

        for t, (cur, prev, i) in enumerate(tiles):
            q = load(qs_ref, cur).astype(BF16)
            s_ref[t, :, :blk] = lax.dot_general(q, load(ks_ref, cur).astype(BF16), contract_last,
                                                preferred_element_type=F32) + bias_cur
            if prev is not None:
                s_ref[t, :, blk:] = lax.dot_general(q, load(ks_ref, prev).astype(BF16), contract_last,
                                                    preferred_element_type=F32) + bias_prev

        for t, (cur, prev, i) in enumerate(tiles):
            s_c = s_ref[t, :, :blk]
            mx = jnp.max(s_c, axis=-1, keepdims=True)
            if prev is not None:
                s_p = s_ref[t, :, blk:]
                mx = jnp.maximum(mx, jnp.max(s_p, axis=-1, keepdims=True))
            if first:
                m_new = jnp.broadcast_to(mx, (blk, LANES))
            else:
                m_old = load(m_ref, cur)
                m_new = jnp.maximum(m_old, mx)
            p_c = jnp.exp(s_c - m_new)
            den = jnp.sum(p_c, axis=-1, keepdims=True)
            p_ref[t, :, :blk] = p_c.astype(BF16)
            if prev is not None:
                p_p = jnp.exp(s_p - m_new)
                den = den + jnp.sum(p_p, axis=-1, keepdims=True)
                p_ref[t, :, blk:] = p_p.astype(BF16)
            if first:
                den = jnp.broadcast_to(den, (blk, LANES))
            else:
                alpha = jnp.exp(m_old - m_new)
                a_ref[t] = alpha
                den = alpha * load(l_ref, cur) + den
            store(l_ref, cur, den)
            if not last:
                store(m_ref, cur, m_new)

        for t, (cur, prev, i) in enumerate(tiles):
            acc = jnp.dot(p_ref[t, :, :blk], load(vs_ref, cur).astype(BF16), preferred_element_type=F32)
            if prev is not None:
                acc = acc + jnp.dot(p_ref[t, :, blk:], load(vs_ref, prev).astype(BF16),
                                    preferred_element_type=F32)
            if not first:
                acc = a_ref[t] * load(acc_ref, cur) + acc
            if last:
                out = jnp.dot(scatter_rows, (acc / load(l_ref, cur)).astype(BF16),
                              preferred_element_type=F32)
                o_ref[pl.ds(i * blk, blk), :] = out.astype(o_ref.dtype)
            else:
                store(acc_ref, cur, acc)


def _dilated_attention(q, kv, *, batch, seq):
    m, width = q.shape
    assert ATTN_BLOCK == LANES
    for window, d in DILATED_BRANCHES:
        assert window // d == ATTN_BLOCK and seq % (d * ATTN_BLOCK) == 0
    n_heads = width // HEAD_DIM
    head_block = pl.BlockSpec((seq, HEAD_DIM), lambda b, h: (b, h))
    return pl.pallas_call(
        functools.partial(_attn_body, seq=seq),
        grid=(batch, n_heads),
        in_specs=[head_block, head_block, pl.BlockSpec((seq, HEAD_DIM), lambda b, h: (b, n_heads + h))],
        out_specs=head_block,
        out_shape=jax.ShapeDtypeStruct((m, width), BF16),
        scratch_shapes=[pltpu.VMEM((seq, LANES), F32)] * 6 + [
            pltpu.VMEM((seq // ATTN_BLOCK, ATTN_BLOCK, 2 * ATTN_BLOCK), F32),
            pltpu.VMEM((seq // ATTN_BLOCK, ATTN_BLOCK, 2 * ATTN_BLOCK), BF16),
            pltpu.VMEM((seq // ATTN_BLOCK, ATTN_BLOCK, LANES), F32)],

        compiler_params=_params(("arbitrary", "arbitrary"), 32),
        name="dilated_attention",
    )(q, kv, kv)


def _router_body(h_ref, gain_ref, r_ref, idx_ref, gate_ref):
    x = h_ref[...]
    hn = x * _rms_scale(x) * gain_ref[...]
    logits = jnp.dot(hn, r_ref[...], preferred_element_type=F32, precision=lax.Precision.HIGHEST)
    lane = lax.broadcasted_iota(jnp.int32, logits.shape, 1).astype(F32)
    logits = jnp.where(lane < N_EXPERTS, logits, NEG)
    m1 = jnp.max(logits, axis=-1, keepdims=True)
    i1 = jnp.min(jnp.where(logits == m1, lane, float(LANES)), axis=-1, keepdims=True)
    rest = jnp.where(lane == i1, NEG, logits)
    m2 = jnp.max(rest, axis=-1, keepdims=True)
    i2 = jnp.min(jnp.where(rest == m2, lane, float(LANES)), axis=-1, keepdims=True)
    e2 = jnp.exp(m2 - m1)
    g1 = 1.0 / (1.0 + e2)
    g2 = e2 / (1.0 + e2)
    idx_ref[...] = jnp.where(lane == 0, i1, jnp.where(lane == 1, i2, 0.0)).astype(jnp.int32)
    gate_ref[...] = jnp.where(lane == 0, g1, jnp.where(lane == 1, g2, 0.0))


def _router(h, gain, router):
    m, d = h.shape
    n_e = router.shape[1]
    assert n_e == N_EXPERTS
    tm = _tile(m, 256, 8)
    r_pad = jnp.zeros((d, LANES), F32).at[:, :n_e].set(router)
    row = pl.BlockSpec((tm, d), lambda i: (i, 0))
    small = pl.BlockSpec((tm, LANES), lambda i: (i, 0))
    return pl.pallas_call(
        _router_body,
        grid=(m // tm,),
        in_specs=[row, pl.BlockSpec((1, d), lambda i: (0, 0)), pl.BlockSpec((d, LANES), lambda i: (0, 0))],
        out_specs=[small, small],
        out_shape=[jax.ShapeDtypeStruct((m, LANES), jnp.int32), jax.ShapeDtypeStruct((m, LANES), F32)],
        compiler_params=_params(("arbitrary",), 32),
        name="router_top2",
    )(h, gain.reshape(1, d), r_pad)


def _dispatch_plan(top_idx, tm):
    m = top_idx.shape[0]
    n_tiles = (m * TOP_K + N_EXPERTS * (tm - 1)) // tm
    flat_e = top_idx.reshape(-1)
    onehot = (flat_e[:, None] == jnp.arange(N_EXPERTS)[None, :]).astype(jnp.int32)
    rank = jnp.sum((jnp.cumsum(onehot, axis=0) - 1) * onehot, axis=1)
    counts = jnp.sum(onehot, axis=0)
    tiles_per = (counts + tm - 1) // tm
    tile_end = jnp.cumsum(tiles_per)
    row_start = (tile_end - tiles_per) * tm
    dest = row_start[flat_e] + rank
    src = jnp.zeros((n_tiles * tm,), jnp.int32).at[dest].set(jnp.arange(m * TOP_K, dtype=jnp.int32) // TOP_K)
    n_used = tile_end[-1]
    t = jnp.minimum(jnp.arange(n_tiles, dtype=jnp.int32), n_used - 1)
    tile_expert = jnp.sum((t[:, None] >= tile_end[None, :]).astype(jnp.int32), axis=1)
    first_tile = (tile_end - tiles_per)[tile_expert]
    tile_first = (t == first_tile).astype(jnp.int32)
    rows_left = counts[tile_expert] - (t - first_tile) * tm
    tile_live = jnp.minimum((rows_left + ROW_QUANTUM - 1) // ROW_QUANTUM * ROW_QUANTUM, tm)
    tile_live = jnp.where(jnp.arange(n_tiles) < n_used, tile_live, 0).astype(jnp.int32)
    return (src, dest.reshape(m, TOP_K).astype(jnp.int32), tile_expert.astype(jnp.int32), t,
            tile_first, tile_live, (n_used * tm).astype(jnp.int32).reshape(1))


DMA_LOOP_UNROLL = 8


def _row_dma_loop(n, copy_of, action):
    def step(j, c):
        getattr(copy_of(j), action)()
        return c
    lax.fori_loop(0, n, step, 0, unroll=DMA_LOOP_UNROLL)


DMA_CHUNKS = 8


def _pipelined_row_tiles(i, cur_used, next_used, copy, n_copies, consume, skip):
    per = n_copies // DMA_CHUNKS

    @pl.when((i == 0) & cur_used)
    def _():
        _row_dma_loop(n_copies, lambda q: copy(0, 0, q), "start")

    if skip is not None:
        pl.when(jnp.logical_not(cur_used))(skip)

    for s in range(2):
        for prefetch in (True, False):
            @pl.when(cur_used & (i % 2 == s) & (next_used if prefetch else jnp.logical_not(next_used)))
            def _():
                for c in range(DMA_CHUNKS):
                    if prefetch:
                        for q in range(c * per, (c + 1) * per):
                            copy(i + 1, 1 - s, q).start()
                    for q in range(c * per, (c + 1) * per):
                        copy(i, s, q).wait()
                    consume(s, c)


def _gather_body(src_ref, nrows_ref, gain_ref, h_hbm, o_ref, buf0_ref, buf1_ref, sems, *, rows):
    i = pl.program_id(0)
    bufs = (buf0_ref, buf1_ref)
    per = rows // DMA_CHUNKS

    def tile_used(t):
        return t * rows < nrows_ref[0]

    def copy(t, s, q):
        return pltpu.make_async_copy(h_hbm.at[pl.ds(src_ref[t * rows + q], 1)], bufs[s].at[pl.ds(q, 1)],
                                     sems.at[s, q // per])

    def consume(s, c):
        r = pl.ds(c * per, per)
        x = bufs[s][r, :]
        o_ref[r, :] = (x * _rms_scale(x) * gain_ref[...]).astype(BF16)

    def skip():
        o_ref[...] = jnp.zeros(o_ref.shape, o_ref.dtype)

    _pipelined_row_tiles(i, tile_used(i), (i + 1 < pl.num_programs(0)) & tile_used(i + 1), copy, rows,
                         consume, skip)


def _gather_norm_rows(h, gain, src, n_rows_used, rows):
    p = src.shape[0]
    d = h.shape[1]
    return pl.pallas_call(
        functools.partial(_gather_body, rows=rows),
        grid_spec=pltpu.PrefetchScalarGridSpec(
            num_scalar_prefetch=2,
            grid=(p // rows,),
            in_specs=[pl.BlockSpec((1, d), lambda i, src, nr: (0, 0)), pl.BlockSpec(memory_space=pl.ANY)],
            out_specs=pl.BlockSpec((rows, d), lambda i, src, nr: (i, 0)),
            scratch_shapes=[pltpu.VMEM((rows, d), F32), pltpu.VMEM((rows, d), F32),
                            pltpu.SemaphoreType.DMA((2, DMA_CHUNKS))],
        ),
        out_shape=jax.ShapeDtypeStruct((p, d), BF16),
        compiler_params=_params(("arbitrary",), 40),
        name="moe_gather_rows",
    )(src, n_rows_used, gain.reshape(1, d), h)


def _expert_changed(pf, n, m):
    return pf[2][m] == 1


def _tile_live_rows(pf, n, m):
    return pf[3][m]


def _moe_gate_up(xs, w_gate_up, plan, tm):
    p, k = xs.shape
    dff = w_gate_up.shape[2] // 2
    tn = _tile(dff, 512, LANES)
    nt = dff // tn

    def epilogue(acc, e_refs, o_refs, pf, n, mi, rows):
        o_refs[0][:rows, :] = _silu_mul(acc, tn).astype(BF16)

    return _ws_matmul(
        xs, [w_gate_up, w_gate_up], [],
        grid=(nt, p // tm),
        x_spec=pl.BlockSpec((tm, k), lambda n, mi, te, tr, tf, tv: (tr[mi], 0)),
        w_specs=[pl.BlockSpec((None, k, tn), lambda n, mi, te, tr, tf, tv: (te[mi], 0, n)),
                 pl.BlockSpec((None, k, tn), lambda n, mi, te, tr, tf, tv: (te[mi], 0, n + nt))],
        extra_specs=[],
        out_shape=jax.ShapeDtypeStruct((p, dff), BF16),
        out_specs=pl.BlockSpec((tm, tn), lambda n, mi, te, tr, tf, tv: (mi, n)),
        epilogue=epilogue, tn=tn, prefetch=plan, recast=_expert_changed, live_rows=_tile_live_rows,
        vmem_mib=56, name="moe_gate_up")


def _moe_down(hs, w_down, plan, tm):
    p, k = hs.shape
    d = w_down.shape[2]
    tn = _tile(d, 1024, LANES)

    def epilogue(acc, e_refs, o_refs, pf, n, mi, rows):
        o_refs[0][:rows, :] = acc

    return _ws_matmul(
        hs, [w_down], [],
        grid=(d // tn, p // tm),
        x_spec=pl.BlockSpec((tm, k), lambda n, mi, te, tr, tf, tv: (tr[mi], 0)),
        w_specs=[pl.BlockSpec((None, k, tn), lambda n, mi, te, tr, tf, tv: (te[mi], 0, n))],
        extra_specs=[],
        out_shape=jax.ShapeDtypeStruct((p, d), F32),
        out_specs=pl.BlockSpec((tm, tn), lambda n, mi, te, tr, tf, tv: (mi, n)),
        epilogue=epilogue, tn=tn, prefetch=plan, recast=_expert_changed, live_rows=_tile_live_rows,
        vmem_mib=56, name="moe_down")


def _combine_body(pos_ref, h_ref, gate_ref, gain_ref, ys_hbm, h_out_ref, hn_ref, buf0_ref, buf1_ref, sems,
                  *, tm):
    i = pl.program_id(0)
    bufs = (buf0_ref, buf1_ref)
    n_copies = tm * TOP_K
    per = n_copies // DMA_CHUNKS
    tokens_per = per // TOP_K

    def copy(t, s, q):
        return pltpu.make_async_copy(ys_hbm.at[pl.ds(pos_ref[t * n_copies + q], 1)],
                                     bufs[s].at[q % TOP_K, pl.ds(q // TOP_K, 1)], sems.at[s, q // per])

    def consume(s, c):
        r = pl.ds(c * tokens_per, tokens_per)
        h = h_ref[r, :]
        gates = gate_ref[r, :]
        for k in range(TOP_K):
            h = h + gates[:, k:k + 1] * bufs[s][k, r, :]
        h_out_ref[r, :] = h
        hn_ref[r, :] = (h * _rms_scale(h) * gain_ref[...]).astype(BF16)

    _pipelined_row_tiles(i, i >= 0, i + 1 < pl.num_programs(0), copy, n_copies, consume, None)


def _moe_combine(h, ys, pos, gates, next_gain):
    m, d = h.shape
    tm = _tile(m, 256, 8)
    row = pl.BlockSpec((tm, d), lambda i, pos: (i, 0))
    return pl.pallas_call(
        functools.partial(_combine_body, tm=tm),
        grid_spec=pltpu.PrefetchScalarGridSpec(
            num_scalar_prefetch=1,
            grid=(m // tm,),
            in_specs=[row, pl.BlockSpec((tm, LANES), lambda i, pos: (i, 0)),
                      pl.BlockSpec((1, d), lambda i, pos: (0, 0)), pl.BlockSpec(memory_space=pl.ANY)],
            out_specs=[row, row],
            scratch_shapes=[pltpu.VMEM((TOP_K, tm, d), F32), pltpu.VMEM((TOP_K, tm, d), F32),
                            pltpu.SemaphoreType.DMA((2, DMA_CHUNKS))],
        ),
        out_shape=[jax.ShapeDtypeStruct((m, d), F32), jax.ShapeDtypeStruct((m, d), BF16)],
        compiler_params=_params(("arbitrary",), 48),
        name="moe_combine",
    )(pos.reshape(-1), h, gates, next_gain.reshape(1, d), ys)


def kernel(x, p, pool_norm, pool_w, pool_scale, kv_norm, w_kv, k_norm, attn_norm, w_q, q_norm, w_o,
           ffn_norm, dense_w_gate_up, dense_w_down, moe_router, moe_w_gate_up, moe_w_down, ple_w,
           ple_norm, ple_gate_w):
    batch, seq, d = x.shape
    m = batch * seq
    p_bf16 = p.astype(BF16).reshape(p.shape[0], m, p.shape[-1])
    cos, sin_signed = _rope_tables(seq)

    h, hn = _pool_layer(x.reshape(m, d), seq, pool_norm[0], pool_w[0], pool_scale[0], ffn_norm[0])
    act = _gate_up(hn, dense_w_gate_up[0])
    h = _matmul_residual(act, dense_w_down[0], h, name="dense_down")
    (hn,) = _rms_cast(h, [ple_norm[0]])
    h = _ple(hn, ple_gate_w, p_bf16, ple_w, h, 0)

    hn_kv, hn_q = _rms_cast(h, [kv_norm, attn_norm[0]])
    width = w_q.shape[2]
    kv = _qk_projection(hn_kv, w_kv, k_norm, cos, sin_signed, seq, n_rope_cols=width,
                        post_scale=1.0, name="kv_proj")
    q = _qk_projection(hn_q, w_q[0], q_norm[0], cos, sin_signed, seq, n_rope_cols=width,
                       post_scale=1.0 / math.sqrt(HEAD_DIM), name="q_proj")

    attn = _dilated_attention(q, kv, batch=batch, seq=seq)
    h = _matmul_residual(attn, w_o[0], h, name="attn_out_proj")

    top_idx, gates = _router(h, ffn_norm[1], moe_router[0])
    tm_moe = _tile(m, 512, 8)
    src, pos, tile_expert, tile_row, tile_first, tile_live, n_rows_used = _dispatch_plan(
        top_idx[:, :TOP_K], tm_moe)
    plan = (tile_expert, tile_row, tile_first, tile_live)
    xs = _gather_norm_rows(h, ffn_norm[1], src, n_rows_used, tm_moe)
    hs = _moe_gate_up(xs, moe_w_gate_up[0], plan, tm_moe)
    ys = _moe_down(hs, moe_w_down[0], plan, tm_moe)
    h, hn = _moe_combine(h, ys, pos, gates, ple_norm[1])
    h = _ple(hn, ple_gate_w, p_bf16, ple_w, h, 1)
    return h.reshape(batch, seq, d)
```

```python
import functools
import math

import jax
import jax.numpy as jnp
from jax import lax
from jax.experimental import pallas as pl
from jax.experimental.pallas import tpu as pltpu

F32 = jnp.float32
BF16 = jnp.bfloat16

HEAD_DIM = 128
LANES = 128
ROPE_THETA = 10000.0
RMS_EPS = 1e-6
POOL_WINDOWS = (2, 4, 8, 16)
POOL_HALO = 32
DILATED_BRANCHES = ((128, 1), (512, 4), (2048, 16))
ATTN_BLOCK = 128
N_EXPERTS = 8
TOP_K = 2
NEG = -1e30
MIB = 1024 * 1024


def _params(semantics, vmem_mib):
    return pltpu.CompilerParams(dimension_semantics=semantics, vmem_limit_bytes=vmem_mib * MIB)


def _tile(dim, pref, quantum):
    if dim <= pref:
        return dim
    t = (pref // quantum) * quantum
    while t > quantum and dim % t:
        t -= quantum
    assert dim % t == 0, (dim, pref, quantum)
    return t


def _rms_scale(x):
    return lax.rsqrt(jnp.mean(x * x, axis=-1, keepdims=True) + RMS_EPS)


def _pool_body(x_ref, halo_ref, w_ref, gain_ref, scale_ref, next_gain_ref,
               h_ref, hn_ref, yn_ref, hrow_ref, lvl_ref, *, ts, seq, n_groups, gdim):
    i = pl.program_id(0)
    g = pl.program_id(1)
    t0 = (i * ts) % seq

    @pl.when(g == 0)
    def _():
        x = x_ref[...]
        y = x * _rms_scale(x) * gain_ref[...]
        xh = halo_ref[...]
        yh = xh * _rms_scale(xh) * gain_ref[...]
        yh = yh * (t0 > 0).astype(F32)
        for gi in range(n_groups):
            yn_ref[gi, :POOL_HALO, :] = yh[:, gi * gdim:(gi + 1) * gdim]
            yn_ref[gi, POOL_HALO:, :] = y[:, gi * gdim:(gi + 1) * gdim]

    pos = t0 + lax.broadcasted_iota(jnp.int32, (ts, 1), 0)
    for gi, win in enumerate(POOL_WINDOWS):
        @pl.when(g == gi)
        def _():
            levels = win.bit_length() - 1
            assert win == 1 << levels and 8 * levels <= POOL_HALO
            src = yn_ref.at[gi]
            for lv in range(1, levels + 1):
                shift, lo = 1 << (lv - 1), 8 * lv
                dst = lvl_ref.at[lv % 2]
                dst[lo:, :] = src[lo:, :] + src[lo - shift:ts + POOL_HALO - shift, :]
                src = dst
            cur = yn_ref[gi, POOL_HALO:, :]
            acc = src[POOL_HALO:, :]
            inv_cnt = 1.0 / jnp.minimum(pos + 1, win).astype(F32)
            pooled = (acc * inv_cnt - cur).astype(BF16)
            sl = slice(gi * gdim, (gi + 1) * gdim)
            mixed = jnp.dot(pooled, w_ref[gi], preferred_element_type=F32)
            hrow_ref[gi] = x_ref[:, sl] + mixed * scale_ref[:, sl]

    h_ref[...] = hrow_ref[g]

    @pl.when(g == n_groups - 1)
    def _():
        ss = jnp.zeros((ts, 1), F32)
        for gi in range(n_groups):
            hg = hrow_ref[gi]
            ss = ss + jnp.sum(hg * hg, axis=-1, keepdims=True)
        inv = lax.rsqrt(ss / (n_groups * gdim) + RMS_EPS)
        for gi in range(n_groups):
            sl = slice(gi * gdim, (gi + 1) * gdim)
            hn_ref[:, sl] = (hrow_ref[gi] * inv * next_gain_ref[:, sl]).astype(BF16)


def _pool_layer(x2, seq, norm_gain, w_groups, scale, next_gain):
    m, d = x2.shape
    n_groups, gdim = w_groups.shape[0], w_groups.shape[1]
    assert n_groups == len(POOL_WINDOWS) and n_groups * gdim == d
    ts = _tile(seq, 256, POOL_HALO)
    hpt = ts // POOL_HALO
    body = functools.partial(_pool_body, ts=ts, seq=seq, n_groups=n_groups, gdim=gdim)
    return pl.pallas_call(
        body,
        grid=(m // ts, n_groups),
        in_specs=[
            pl.BlockSpec((ts, d), lambda i, g: (i, 0)),
            pl.BlockSpec((POOL_HALO, d), lambda i, g: (jnp.maximum(i * hpt - 1, 0), 0)),
            pl.BlockSpec((n_groups, gdim, gdim), lambda i, g: (0, 0, 0)),
            pl.BlockSpec((1, d), lambda i, g: (0, 0)),
            pl.BlockSpec((1, d), lambda i, g: (0, 0)),
            pl.BlockSpec((1, d), lambda i, g: (0, 0)),
        ],
        out_specs=[
            pl.BlockSpec((ts, gdim), lambda i, g: (i, g)),
            pl.BlockSpec((ts, d), lambda i, g: (i, 0)),
        ],
        out_shape=[jax.ShapeDtypeStruct((m, d), F32), jax.ShapeDtypeStruct((m, d), BF16)],
        scratch_shapes=[
            pltpu.VMEM((n_groups, ts + POOL_HALO, gdim), F32),
            pltpu.VMEM((n_groups, ts, gdim), F32),
            pltpu.VMEM((2, ts + POOL_HALO, gdim), F32),
        ],
        compiler_params=_params(("arbitrary", "arbitrary"), 48),
        name="pool_layer",
    )(x2, x2, w_groups.astype(BF16), norm_gain.reshape(1, d), scale.reshape(1, d), next_gain.reshape(1, d))


def _rms_body(x_ref, *refs, n_out):
    gains, outs = refs[:n_out], refs[n_out:]
    x = x_ref[...]
    y = x * _rms_scale(x)
    for g_ref, o_ref in zip(gains, outs):
        o_ref[...] = (y * g_ref[...]).astype(BF16)


def _rms_cast(x2, gains):
    m, d = x2.shape
    tm = _tile(m, 256, 8)
    n_out = len(gains)
    row = pl.BlockSpec((tm, d), lambda i: (i, 0))
    vec = pl.BlockSpec((1, d), lambda i: (0, 0))
    return pl.pallas_call(
        functools.partial(_rms_body, n_out=n_out),
        grid=(m // tm,),
        in_specs=[row] + [vec] * n_out,
        out_specs=[row] * n_out,
        out_shape=[jax.ShapeDtypeStruct((m, d), BF16)] * n_out,
        compiler_params=_params(("arbitrary",), 32),
        name="rms_cast",
    )(x2, *[g.reshape(1, d) for g in gains])


def _ws_matmul(x, ws, extras, *, grid, x_spec, w_specs, extra_specs, out_shape, out_specs,
               epilogue, tn, recast, k_block=None, prefetch=(), valid=None, vmem_mib=48,
               name="ws_matmul"):
    n_w, n_e, n_pf = len(ws), len(extras), len(prefetch)
    outs = out_shape if isinstance(out_shape, (list, tuple)) else [out_shape]
    n_out = len(outs)
    k = x.shape[1] if k_block is None else k_block
    cast_rows = _tile(k, 512, 16)

    def body(*refs):
        pf, refs = refs[:n_pf], refs[n_pf:]
        x_ref = refs[0]
        w_refs = refs[1:1 + n_w]
        e_refs = refs[1 + n_w:1 + n_w + n_e]
        o_refs = refs[1 + n_w + n_e:1 + n_w + n_e + n_out]
        wb_ref = refs[1 + n_w + n_e + n_out]
        n, m = pl.program_id(0), pl.program_id(1)

        def compute():
            @pl.when(recast(pf, n, m))
            def _():
                def chunk(c, carry):
                    r0 = pl.multiple_of(c * cast_rows, cast_rows)
                    for j, w_ref in enumerate(w_refs):
                        wb_ref[pl.ds(r0, cast_rows), j * tn:(j + 1) * tn] = (
                            w_ref[pl.ds(r0, cast_rows), :].astype(BF16))
                    return carry
                lax.fori_loop(0, k // cast_rows, chunk, 0)

            acc = jnp.dot(x_ref[...], wb_ref[...], preferred_element_type=F32)
            epilogue(acc, e_refs, o_refs, pf, n, m)

        if valid is None:
            compute()
        else:
            ok = valid(pf, n, m)
            pl.when(ok)(compute)

            @pl.when(jnp.logical_not(ok))
            def _():
                for o_ref in o_refs:
                    o_ref[...] = jnp.zeros(o_ref.shape, o_ref.dtype)

    return pl.pallas_call(
        body,
        grid_spec=pltpu.PrefetchScalarGridSpec(
            num_scalar_prefetch=n_pf,
            grid=grid,
            in_specs=[x_spec] + list(w_specs) + list(extra_specs),
            out_specs=out_specs,
            scratch_shapes=[pltpu.VMEM((k, n_w * tn), BF16)],
        ),
        out_shape=out_shape,
        compiler_params=_params(("arbitrary", "arbitrary"), vmem_mib),
        name=name,
    )(*prefetch, x, *ws, *extras)


def _first_row_tile(pf, n, m):
    return m == 0


def _silu_mul(acc, tn):
    gate, up = acc[:, :tn], acc[:, tn:]
    return gate * jax.nn.sigmoid(gate) * up


def _gate_up(xn, w_gate_up):
    m, k = xn.shape
    dff = w_gate_up.shape[1] // 2
    tm, tn = _tile(m, 1024, 8), _tile(dff, 256, LANES)
    nt = dff // tn

    def epilogue(acc, e_refs, o_refs, pf, n, mi):
        o_refs[0][...] = _silu_mul(acc, tn).astype(BF16)

    return _ws_matmul(
        xn, [w_gate_up, w_gate_up], [],
        grid=(nt, m // tm),
        x_spec=pl.BlockSpec((tm, k), lambda n, mi: (mi, 0)),
        w_specs=[pl.BlockSpec((k, tn), lambda n, mi: (0, n)),
                 pl.BlockSpec((k, tn), lambda n, mi: (0, n + nt))],
        extra_specs=[],
        out_shape=jax.ShapeDtypeStruct((m, dff), BF16),
        out_specs=pl.BlockSpec((tm, tn), lambda n, mi: (mi, n)),
        epilogue=epilogue, tn=tn, recast=_first_row_tile, name="dense_gate_up")


MAX_WEIGHT_TILE_ROWS = 6144


def _matmul_residual(x, w, res, *, name):
    m, k = x.shape
    n_out = w.shape[1]
    n_chunks = pl.cdiv(k, MAX_WEIGHT_TILE_ROWS)
    assert k % (n_chunks * LANES) == 0
    kc = k // n_chunks
    tm, tn = _tile(m, 1024 if kc <= 4096 else 512, 8), _tile(n_out, 512, LANES)

    def epilogue(acc, e_refs, o_refs, pf, n, mi):
        o_refs[0][...] = e_refs[0][...] + acc

    for c in range(n_chunks):
        res = _ws_matmul(
            x, [w], [res],
            grid=(n_out // tn, m // tm),
            x_spec=pl.BlockSpec((tm, kc), lambda n, mi, c=c: (mi, c)),
            w_specs=[pl.BlockSpec((kc, tn), lambda n, mi, c=c: (c, n))],
            extra_specs=[pl.BlockSpec((tm, tn), lambda n, mi: (mi, n))],
            out_shape=jax.ShapeDtypeStruct((m, n_out), F32),
            out_specs=pl.BlockSpec((tm, tn), lambda n, mi: (mi, n)),
            epilogue=epilogue, tn=tn, k_block=kc, recast=_first_row_tile, vmem_mib=56,
            name=name)
    return res


def _ple(hn, w_gate, p_bf16, w_ple, h, layer):
    m, k = hn.shape
    d = w_gate.shape[2]
    pdim = p_bf16.shape[2]
    tm, tn = _tile(m, 1024, 8), _tile(d, 512, LANES)

    def epilogue(acc, e_refs, o_refs, pf, n, mi):
        p_ref, wp_ref, h_ref = e_refs
        emb = jnp.dot(p_ref[...], wp_ref[...].astype(BF16), preferred_element_type=F32)
        o_refs[0][...] = h_ref[...] + emb * jax.nn.sigmoid(acc)

    return _ws_matmul(
        hn, [w_gate], [p_bf16, w_ple, h],
        grid=(d // tn, m // tm),
        x_spec=pl.BlockSpec((tm, k), lambda n, mi: (mi, 0)),
        w_specs=[pl.BlockSpec((None, k, tn), lambda n, mi: (layer, 0, n))],
        extra_specs=[pl.BlockSpec((None, tm, pdim), lambda n, mi: (layer, mi, 0)),
                     pl.BlockSpec((None, pdim, tn), lambda n, mi: (layer, 0, n)),
                     pl.BlockSpec((tm, tn), lambda n, mi: (mi, n))],
        out_shape=jax.ShapeDtypeStruct((m, d), F32),
        out_specs=pl.BlockSpec((tm, tn), lambda n, mi: (mi, n)),
        epilogue=epilogue, tn=tn, recast=_first_row_tile, vmem_mib=56, name="ple")


def _rope_tables(seq, head_gain, post_scale):
    half = HEAD_DIM // 2
    inv_freq = jnp.exp(-math.log(ROPE_THETA) * jnp.arange(half, dtype=F32) / half)
    ang = jnp.arange(seq, dtype=F32)[:, None] * inv_freq[None, :]
    cos, sin = jnp.cos(ang), jnp.sin(ang)
    a = jnp.concatenate([cos, cos], axis=-1) * head_gain[None, :] * post_scale
    b = jnp.concatenate([-sin, sin], axis=-1) * jnp.roll(head_gain, half)[None, :] * post_scale
    return a, b


def _norm_rope_heads(acc, rope_a, rope_b, out_ref, tn):
    for hh in range(tn // HEAD_DIM):
        sl = slice(hh * HEAD_DIM, (hh + 1) * HEAD_DIM)
        y = acc[:, sl]
        y = (y * rope_a + pltpu.roll(y, HEAD_DIM // 2, axis=1) * rope_b) * _rms_scale(y)
        out_ref[:, sl] = y.astype(out_ref.dtype)


def _qk_projection(xn, w, rope_a, rope_b, seq, *, n_rope_cols, name):
    m, k = xn.shape
    n_out = w.shape[1]
    tm, tn = _tile(seq, 1024, 8), _tile(min(n_out, n_rope_cols), 512, LANES)
    n_rope_tiles = n_rope_cols // tn
    spb = seq // tm

    def epilogue(acc, e_refs, o_refs, pf, n, mi):
        a_ref, b_ref = e_refs

        @pl.when(n < n_rope_tiles)
        def _():
            _norm_rope_heads(acc, a_ref[...], b_ref[...], o_refs[0], tn)

        if n_rope_tiles * tn < n_out:
            @pl.when(n >= n_rope_tiles)
            def _():
                o_refs[0][...] = acc.astype(BF16)

    return _ws_matmul(
        xn, [w], [rope_a, rope_b],
        grid=(n_out // tn, m // tm),
        x_spec=pl.BlockSpec((tm, k), lambda n, mi: (mi, 0)),
        w_specs=[pl.BlockSpec((k, tn), lambda n, mi: (0, n))],
        extra_specs=[pl.BlockSpec((tm, HEAD_DIM), lambda n, mi: (mi % spb, 0)),
                     pl.BlockSpec((tm, HEAD_DIM), lambda n, mi: (mi % spb, 0))],
        out_shape=jax.ShapeDtypeStruct((m, n_out), BF16),
        out_specs=pl.BlockSpec((tm, tn), lambda n, mi: (mi, n)),
        epilogue=epilogue, tn=tn, recast=_first_row_tile, vmem_mib=56, name=name)


def _attn_body(q_ref, k_ref, v_ref, o_ref, qs_ref, ks_ref, vs_ref, acc_ref, m_ref, l_ref,
               s_ref, p_ref, a_ref, *, seq):
    blk = ATTN_BLOCK
    regroup = max(d for _, d in DILATED_BRANCHES)
    res_rows = seq // regroup
    row = lax.broadcasted_iota(jnp.int32, (blk, blk), 0)
    col = lax.broadcasted_iota(jnp.int32, (blk, blk), 1)
    contract_last = (((1,), (1,)), ((), ()))
    order = sorted(DILATED_BRANCHES, key=lambda wd: -wd[1])
    assert order[-1][1] == 1 and regroup % 8 == 0 and blk % regroup == 0

    def seq_pos(idx, pieces):
        n = blk // pieces
        return pieces * (idx % n) + idx // n

    def pieces_of(d, c, i):
        n = blk * d // regroup
        return [((c + d * a) * res_rows + n * i, n) for a in range(regroup // d)]

    def load(ref, pieces):
        parts = [ref[pl.ds(s, n), :] for s, n in pieces]
        return parts[0] if len(parts) == 1 else jnp.concatenate(parts, axis=0)

    def store(ref, pieces, val):
        off = 0
        for s, n in pieces:
            ref[pl.ds(s, n), :] = val[off:off + n]
            off += n

    gather_rows = (col == seq_pos(row, regroup)).astype(BF16)
    scatter_rows = (row == seq_pos(col, regroup)).astype(BF16)
    for i in range(seq // blk):
        for src, dst in ((q_ref, qs_ref), (k_ref, ks_ref), (v_ref, vs_ref)):
            regrouped = jnp.dot(gather_rows, src[pl.ds(i * blk, blk), :], preferred_element_type=F32)
            store(dst, pieces_of(1, 0, i), regrouped)

    for step, (window, d) in enumerate(order):
        first, last = step == 0, step == len(order) - 1
        pieces = regroup // d
        pos_r, pos_c = seq_pos(row, pieces), seq_pos(col, pieces)
        bias_cur = jnp.where(pos_c <= pos_r, 0.0, NEG).astype(F32)
        bias_prev = jnp.where(pos_c >= pos_r, 0.0, NEG).astype(F32)
        tiles = [(pieces_of(d, c, i), pieces_of(d, c, i - 1) if i > 0 else None, i)
                 for c in range(d) for i in range(seq // d // blk)]

        for t, (cur, prev, i) in enumerate(tiles):
            q = load(qs_ref, cur).astype(BF16)
            s_ref[t, :, :blk] = lax.dot_general(q, load(ks_ref, cur).astype(BF16), contract_last,
                                                preferred_element_type=F32) + bias_cur
            if prev is not None:
                s_ref[t, :, blk:] = lax.dot_general(q, load(ks_ref, prev).astype(BF16), contract_last,
                                                    preferred_element_type=F32) + bias_prev

        for t, (cur, prev, i) in enumerate(tiles):
            s_c = s_ref[t, :, :blk]
            mx = jnp.max(s_c, axis=-1, keepdims=True)
            if prev is not None:
                s_p = s_ref[t, :, blk:]
                mx = jnp.maximum(mx, jnp.max(s_p, axis=-1, keepdims=True))
            if first:
                m_new = jnp.broadcast_to(mx, (blk, LANES))
            else:
                m_old = load(m_ref, cur)
                m_new = jnp.maximum(m_old, mx)
            p_c = jnp.exp(s_c - m_new)
            den = jnp.sum(p_c, axis=-1, keepdims=True)
            p_ref[t, :, :blk] = p_c.astype(BF16)
            if prev is not None:
                p_p = jnp.exp(s_p - m_new)
                den = den + jnp.sum(p_p, axis=-1, keepdims=True)
                p_ref[t, :, blk:] = p_p.astype(BF16)
            if first:
                den = jnp.broadcast_to(den, (blk, LANES))
            else:
                alpha = jnp.exp(m_old - m_new)
                a_ref[t] = alpha
                den = alpha * load(l_ref, cur) + den
            store(l_ref, cur, den)
            if not last:
                store(m_ref, cur, m_new)

        for t, (cur, prev, i) in enumerate(tiles):
            acc = jnp.dot(p_ref[t, :, :blk], load(vs_ref, cur).astype(BF16), preferred_element_type=F32)
            if prev is not None:
                acc = acc + jnp.dot(p_ref[t, :, blk:], load(vs_ref, prev).astype(BF16),
                                    preferred_element_type=F32)
            if not first:
                acc = a_ref[t] * load(acc_ref, cur) + acc
            if last:
                out = jnp.dot(scatter_rows, (acc / load(l_ref, cur)).astype(BF16),
                              preferred_element_type=F32)
                o_ref[pl.ds(i * blk, blk), :] = out.astype(o_ref.dtype)
            else:
                store(acc_ref, cur, acc)


def _dilated_attention(q, kv, *, batch, seq):
    m, width = q.shape
    assert ATTN_BLOCK == LANES
    for window, d in DILATED_BRANCHES:
        assert window // d == ATTN_BLOCK and seq % (d * ATTN_BLOCK) == 0
    n_heads = width // HEAD_DIM
    head_block = pl.BlockSpec((seq, HEAD_DIM), lambda b, h: (b, h))
    return pl.pallas_call(
        functools.partial(_attn_body, seq=seq),
        grid=(batch, n_heads),
        in_specs=[head_block, head_block, pl.BlockSpec((seq, HEAD_DIM), lambda b, h: (b, n_heads + h))],
        out_specs=head_block,
        out_shape=jax.ShapeDtypeStruct((m, width), BF16),
        scratch_shapes=[pltpu.VMEM((seq, LANES), F32)] * 6 + [
            pltpu.VMEM((seq // ATTN_BLOCK, ATTN_BLOCK, 2 * ATTN_BLOCK), F32),
            pltpu.VMEM((seq // ATTN_BLOCK, ATTN_BLOCK, 2 * ATTN_BLOCK), BF16),
            pltpu.VMEM((seq // ATTN_BLOCK, ATTN_BLOCK, LANES), F32)],
        compiler_params=_params(("arbitrary", "arbitrary"), 32),
        name="dilated_attention",
    )(q, kv, kv)


def _router_body(h_ref, rg_ref, idx_ref, gate_ref):
    x = h_ref[...]
    inv = _rms_scale(x)
    lane = lax.broadcasted_iota(jnp.int32, (x.shape[0], LANES), 1).astype(F32)
    logits = jnp.full((x.shape[0], LANES), NEG, F32)
    for e in range(N_EXPERTS):
        logit = jnp.sum(x * rg_ref[e:e + 1, :], axis=-1, keepdims=True) * inv
        logits = jnp.where(lane == e, logit, logits)
    m1 = jnp.max(logits, axis=-1, keepdims=True)
    i1 = jnp.min(jnp.where(logits == m1, lane, float(LANES)), axis=-1, keepdims=True)
    rest = jnp.where(lane == i1, NEG, logits)
    m2 = jnp.max(rest, axis=-1, keepdims=True)
    i2 = jnp.min(jnp.where(rest == m2, lane, float(LANES)), axis=-1, keepdims=True)
    e2 = jnp.exp(m2 - m1)
    g1 = 1.0 / (1.0 + e2)
    g2 = e2 / (1.0 + e2)
    idx_ref[...] = jnp.where(lane == 0, i1, jnp.where(lane == 1, i2, 0.0)).astype(jnp.int32)
    gate_ref[...] = jnp.where(lane == 0, g1, jnp.where(lane == 1, g2, 0.0))


def _router(h, gain, router):
    m, d = h.shape
    n_e = router.shape[1]
    assert n_e == N_EXPERTS
    tm = _tile(m, 256, 8)
    gained_router = router.T * gain[None, :]
    row = pl.BlockSpec((tm, d), lambda i: (i, 0))
    small = pl.BlockSpec((tm, LANES), lambda i: (i, 0))
    return pl.pallas_call(
        _router_body,
        grid=(m // tm,),
        in_specs=[row, pl.BlockSpec((n_e, d), lambda i: (0, 0))],
        out_specs=[small, small],
        out_shape=[jax.ShapeDtypeStruct((m, LANES), jnp.int32), jax.ShapeDtypeStruct((m, LANES), F32)],
        compiler_params=_params(("arbitrary",), 32),
        name="router_top2",
    )(h, gained_router)


def _dispatch_plan(top_idx, tm):
    m = top_idx.shape[0]
    n_tiles = (m * TOP_K + N_EXPERTS * (tm - 1)) // tm
    flat_e = top_idx.reshape(-1)
    onehot = (flat_e[:, None] == jnp.arange(N_EXPERTS)[None, :]).astype(jnp.int32)
    rank = jnp.sum((jnp.cumsum(onehot, axis=0) - 1) * onehot, axis=1)
    counts = jnp.sum(onehot, axis=0)
    tiles_per = (counts + tm - 1) // tm
    tile_end = jnp.cumsum(tiles_per)
    row_start = (tile_end - tiles_per) * tm
    dest = row_start[flat_e] + rank
    src = jnp.zeros((n_tiles * tm,), jnp.int32).at[dest].set(jnp.arange(m * TOP_K, dtype=jnp.int32) // TOP_K)
    n_used = tile_end[-1]
    t = jnp.minimum(jnp.arange(n_tiles, dtype=jnp.int32), n_used - 1)
    tile_expert = jnp.sum((t[:, None] >= tile_end[None, :]).astype(jnp.int32), axis=1)
    tile_first = (t == (tile_end - tiles_per)[tile_expert]).astype(jnp.int32)
    tile_valid = (jnp.arange(n_tiles) < n_used).astype(jnp.int32)
    return (src, dest.reshape(m, TOP_K).astype(jnp.int32), tile_expert.astype(jnp.int32), t,
            tile_first, tile_valid, (n_used * tm).astype(jnp.int32).reshape(1))


DMA_LOOP_UNROLL = 8


def _row_dma_loop(n, copy_of, action):
    def step(j, c):
        getattr(copy_of(j), action)()
        return c
    lax.fori_loop(0, n, step, 0, unroll=DMA_LOOP_UNROLL)


DMA_CHUNKS = 8


def _pipelined_row_tiles(i, cur_used, next_used, copy, n_copies, consume, skip):
    per = n_copies // DMA_CHUNKS

    @pl.when((i == 0) & cur_used)
    def _():
        _row_dma_loop(n_copies, lambda q: copy(0, 0, q), "start")

    if skip is not None:
        pl.when(jnp.logical_not(cur_used))(skip)

    for s in range(2):
        for prefetch in (True, False):
            @pl.when(cur_used & (i % 2 == s) & (next_used if prefetch else jnp.logical_not(next_used)))
            def _():
                for c in range(DMA_CHUNKS):
                    if prefetch:
                        for q in range(c * per, (c + 1) * per):
                            copy(i + 1, 1 - s, q).start()
                    for q in range(c * per, (c + 1) * per):
                        copy(i, s, q).wait()
                    consume(s, c)


def _gather_body(src_ref, nrows_ref, gain_ref, h_hbm, o_ref, buf0_ref, buf1_ref, sems, *, rows):
    i = pl.program_id(0)
    bufs = (buf0_ref, buf1_ref)
    per = rows // DMA_CHUNKS

    def tile_used(t):
        return t * rows < nrows_ref[0]

    def copy(t, s, q):
        return pltpu.make_async_copy(h_hbm.at[pl.ds(src_ref[t * rows + q], 1)], bufs[s].at[pl.ds(q, 1)],
                                     sems.at[s, q // per])

    def consume(s, c):
        r = pl.ds(c * per, per)
        x = bufs[s][r, :]
        o_ref[r, :] = (x * _rms_scale(x) * gain_ref[...]).astype(BF16)

    def skip():
        o_ref[...] = jnp.zeros(o_ref.shape, o_ref.dtype)

    _pipelined_row_tiles(i, tile_used(i), (i + 1 < pl.num_programs(0)) & tile_used(i + 1), copy, rows,
                         consume, skip)


def _gather_norm_rows(h, gain, src, n_rows_used, rows):
    p = src.shape[0]
    d = h.shape[1]
    return pl.pallas_call(
        functools.partial(_gather_body, rows=rows),
        grid_spec=pltpu.PrefetchScalarGridSpec(
            num_scalar_prefetch=2,
            grid=(p // rows,),
            in_specs=[pl.BlockSpec((1, d), lambda i, src, nr: (0, 0)), pl.BlockSpec(memory_space=pl.ANY)],
            out_specs=pl.BlockSpec((rows, d), lambda i, src, nr: (i, 0)),
            scratch_shapes=[pltpu.VMEM((rows, d), F32), pltpu.VMEM((rows, d), F32),
                            pltpu.SemaphoreType.DMA((2, DMA_CHUNKS))],
        ),
        out_shape=jax.ShapeDtypeStruct((p, d), BF16),
        compiler_params=_params(("arbitrary",), 40),
        name="moe_gather_rows",
    )(src, n_rows_used, gain.reshape(1, d), h)


def _expert_changed(pf, n, m):
    return pf[2][m] == 1


def _tile_valid(pf, n, m):
    return pf[3][m] == 1


def _moe_gate_up(xs, w_gate_up, plan, tm):
    p, k = xs.shape
    dff = w_gate_up.shape[2] // 2
    tn = _tile(dff, 512, LANES)
    nt = dff // tn

    def epilogue(acc, e_refs, o_refs, pf, n, mi):
        o_refs[0][...] = _silu_mul(acc, tn).astype(BF16)

    return _ws_matmul(
        xs, [w_gate_up, w_gate_up], [],
        grid=(nt, p // tm),
        x_spec=pl.BlockSpec((tm, k), lambda n, mi, te, tr, tf, tv: (tr[mi], 0)),
        w_specs=[pl.BlockSpec((None, k, tn), lambda n, mi, te, tr, tf, tv: (te[mi], 0, n)),
                 pl.BlockSpec((None, k, tn), lambda n, mi, te, tr, tf, tv: (te[mi], 0, n + nt))],
        extra_specs=[],
        out_shape=jax.ShapeDtypeStruct((p, dff), BF16),
        out_specs=pl.BlockSpec((tm, tn), lambda n, mi, te, tr, tf, tv: (mi, n)),
        epilogue=epilogue, tn=tn, prefetch=plan, recast=_expert_changed, valid=_tile_valid,
        vmem_mib=56, name="moe_gate_up")


def _moe_down(hs, w_down, plan, tm):
    p, k = hs.shape
    d = w_down.shape[2]
    tn = _tile(d, 1024, LANES)

    def epilogue(acc, e_refs, o_refs, pf, n, mi):
        o_refs[0][...] = acc

    return _ws_matmul(
        hs, [w_down], [],
        grid=(d // tn, p // tm),
        x_spec=pl.BlockSpec((tm, k), lambda n, mi, te, tr, tf, tv: (tr[mi], 0)),
        w_specs=[pl.BlockSpec((None, k, tn), lambda n, mi, te, tr, tf, tv: (te[mi], 0, n))],
        extra_specs=[],
        out_shape=jax.ShapeDtypeStruct((p, d), F32),
        out_specs=pl.BlockSpec((tm, tn), lambda n, mi, te, tr, tf, tv: (mi, n)),
        epilogue=epilogue, tn=tn, prefetch=plan, recast=_expert_changed, valid=_tile_valid,
        vmem_mib=56, name="moe_down")


def _combine_body(pos_ref, h_ref, gate_ref, gain_ref, ys_hbm, h_out_ref, hn_ref, buf0_ref, buf1_ref, sems,
                  *, tm):
    i = pl.program_id(0)
    bufs = (buf0_ref, buf1_ref)
    n_copies = tm * TOP_K
    per = n_copies // DMA_CHUNKS
    tokens_per = per // TOP_K

    def copy(t, s, q):
        return pltpu.make_async_copy(ys_hbm.at[pl.ds(pos_ref[t * n_copies + q], 1)],
                                     bufs[s].at[q % TOP_K, pl.ds(q // TOP_K, 1)], sems.at[s, q // per])

    def consume(s, c):
        r = pl.ds(c * tokens_per, tokens_per)
        h = h_ref[r, :]
        gates = gate_ref[r, :]
        for k in range(TOP_K):
            h = h + gates[:, k:k + 1] * bufs[s][k, r, :]
        h_out_ref[r, :] = h
        hn_ref[r, :] = (h * _rms_scale(h) * gain_ref[...]).astype(BF16)

    _pipelined_row_tiles(i, i >= 0, i + 1 < pl.num_programs(0), copy, n_copies, consume, None)


def _moe_combine(h, ys, pos, gates, next_gain):
    m, d = h.shape
    tm = _tile(m, 256, 8)
    row = pl.BlockSpec((tm, d), lambda i, pos: (i, 0))
    return pl.pallas_call(
        functools.partial(_combine_body, tm=tm),
        grid_spec=pltpu.PrefetchScalarGridSpec(
            num_scalar_prefetch=1,
            grid=(m // tm,),
            in_specs=[row, pl.BlockSpec((tm, LANES), lambda i, pos: (i, 0)),
                      pl.BlockSpec((1, d), lambda i, pos: (0, 0)), pl.BlockSpec(memory_space=pl.ANY)],
            out_specs=[row, row],
            scratch_shapes=[pltpu.VMEM((TOP_K, tm, d), F32), pltpu.VMEM((TOP_K, tm, d), F32),
                            pltpu.SemaphoreType.DMA((2, DMA_CHUNKS))],
        ),
        out_shape=[jax.ShapeDtypeStruct((m, d), F32), jax.ShapeDtypeStruct((m, d), BF16)],
        compiler_params=_params(("arbitrary",), 48),
        name="moe_combine",
    )(pos.reshape(-1), h, gates, next_gain.reshape(1, d), ys)


def kernel(x, p, pool_norm, pool_w, pool_scale, kv_norm, w_kv, k_norm, attn_norm, w_q, q_norm, w_o,
           ffn_norm, dense_w_gate_up, dense_w_down, moe_router, moe_w_gate_up, moe_w_down, ple_w,
           ple_norm, ple_gate_w):
    batch, seq, d = x.shape
    m = batch * seq
    p_bf16 = p.astype(BF16).reshape(p.shape[0], m, p.shape[-1])

    h, hn = _pool_layer(x.reshape(m, d), seq, pool_norm[0], pool_w[0], pool_scale[0], ffn_norm[0])
    act = _gate_up(hn, dense_w_gate_up[0])
    h = _matmul_residual(act, dense_w_down[0], h, name="dense_down")
    (hn,) = _rms_cast(h, [ple_norm[0]])
    h = _ple(hn, ple_gate_w, p_bf16, ple_w, h, 0)

    hn_kv, hn_q = _rms_cast(h, [kv_norm, attn_norm[0]])
    width = w_q.shape[2]
    kv = _qk_projection(hn_kv, w_kv, *_rope_tables(seq, k_norm, 1.0), seq, n_rope_cols=width,
                        name="kv_proj")
    q = _qk_projection(hn_q, w_q[0], *_rope_tables(seq, q_norm[0], 1.0 / math.sqrt(HEAD_DIM)), seq,
                       n_rope_cols=width, name="q_proj")

    attn = _dilated_attention(q, kv, batch=batch, seq=seq)
    h = _matmul_residual(attn, w_o[0], h, name="attn_out_proj")

    top_idx, gates = _router(h, ffn_norm[1], moe_router[0])
    tm_moe = _tile(m, 512, 8)
    src, pos, tile_expert, tile_row, tile_first, tile_valid, n_rows_used = _dispatch_plan(
        top_idx[:, :TOP_K], tm_moe)
    plan = (tile_expert, tile_row, tile_first, tile_valid)
    xs = _gather_norm_rows(h, ffn_norm[1], src, n_rows_used, tm_moe)
    hs = _moe_gate_up(xs, moe_w_gate_up[0], plan, tm_moe)
    ys = _moe_down(hs, moe_w_down[0], plan, tm_moe)
    h, hn = _moe_combine(h, ys, pos, gates, ple_norm[1])
    h = _ple(hn, ple_gate_w, p_bf16, ple_w, h, 1)
    return h.reshape(batch, seq, d)
```

```python
import functools
import math

import jax
import jax.numpy as jnp
from jax import lax
from jax.experimental import pallas as pl
from jax.experimental.pallas import tpu as pltpu

F32 = jnp.float32
BF16 = jnp.bfloat16

HEAD_DIM = 128
LANES = 128
ROPE_THETA = 10000.0
RMS_EPS = 1e-6
POOL_WINDOWS = (2, 4, 8, 16)
POOL_HALO = 32
DILATED_BRANCHES = ((128, 1), (512, 4), (2048, 16))
ATTN_BLOCK = 128
N_EXPERTS = 8
TOP_K = 2
NEG = -1e30
MIB = 1024 * 1024


def _params(semantics, vmem_mib):
    return pltpu.CompilerParams(dimension_semantics=semantics, vmem_limit_bytes=vmem_mib * MIB)


def _tile(dim, pref, quantum):
    if dim <= pref:
        return dim
    t = (pref // quantum) * quantum
    while t > quantum and dim % t:
        t -= quantum
    assert dim % t == 0, (dim, pref, quantum)
    return t


def _rms_scale(x):
    return lax.rsqrt(jnp.mean(x * x, axis=-1, keepdims=True) + RMS_EPS)


def _pool_body(x_ref, halo_ref, w_ref, gain_ref, scale_ref, next_gain_ref,
               h_ref, hn_ref, yn_ref, hrow_ref, lvl_ref, *, ts, seq, n_groups, gdim):
    i = pl.program_id(0)
    g = pl.program_id(1)
    t0 = (i * ts) % seq

    @pl.when(g == 0)
    def _():
        x = x_ref[...]
        y = x * _rms_scale(x) * gain_ref[...]
        xh = halo_ref[...]
        yh = xh * _rms_scale(xh) * gain_ref[...]
        yh = yh * (t0 > 0).astype(F32)
        for gi in range(n_groups):
            yn_ref[gi, :POOL_HALO, :] = yh[:, gi * gdim:(gi + 1) * gdim]
            yn_ref[gi, POOL_HALO:, :] = y[:, gi * gdim:(gi + 1) * gdim]

    pos = t0 + lax.broadcasted_iota(jnp.int32, (ts, 1), 0)
    for gi, win in enumerate(POOL_WINDOWS):
        @pl.when(g == gi)
        def _():
            levels = win.bit_length() - 1
            assert win == 1 << levels and 8 * levels <= POOL_HALO
            src = yn_ref.at[gi]
            for lv in range(1, levels + 1):
                shift, lo = 1 << (lv - 1), 8 * lv
                dst = lvl_ref.at[lv % 2]
                dst[lo:, :] = src[lo:, :] + src[lo - shift:ts + POOL_HALO - shift, :]
                src = dst
            cur = yn_ref[gi, POOL_HALO:, :]
            acc = src[POOL_HALO:, :]
            inv_cnt = 1.0 / jnp.minimum(pos + 1, win).astype(F32)
            pooled = (acc * inv_cnt - cur).astype(BF16)
            sl = slice(gi * gdim, (gi + 1) * gdim)
            mixed = jnp.dot(pooled, w_ref[gi], preferred_element_type=F32)
            hrow_ref[gi] = x_ref[:, sl] + mixed * scale_ref[:, sl]

    h_ref[...] = hrow_ref[g]

    @pl.when(g == n_groups - 1)
    def _():
        ss = jnp.zeros((ts, 1), F32)
        for gi in range(n_groups):
            hg = hrow_ref[gi]
            ss = ss + jnp.sum(hg * hg, axis=-1, keepdims=True)
        inv = lax.rsqrt(ss / (n_groups * gdim) + RMS_EPS)
        for gi in range(n_groups):
            sl = slice(gi * gdim, (gi + 1) * gdim)
            hn_ref[:, sl] = (hrow_ref[gi] * inv * next_gain_ref[:, sl]).astype(BF16)


def _pool_layer(x2, seq, norm_gain, w_groups, scale, next_gain):
    m, d = x2.shape
    n_groups, gdim = w_groups.shape[0], w_groups.shape[1]
    assert n_groups == len(POOL_WINDOWS) and n_groups * gdim == d
    ts = _tile(seq, 256, POOL_HALO)
    hpt = ts // POOL_HALO
    body = functools.partial(_pool_body, ts=ts, seq=seq, n_groups=n_groups, gdim=gdim)
    return pl.pallas_call(
        body,
        grid=(m // ts, n_groups),
        in_specs=[
            pl.BlockSpec((ts, d), lambda i, g: (i, 0)),
            pl.BlockSpec((POOL_HALO, d), lambda i, g: (jnp.maximum(i * hpt - 1, 0), 0)),
            pl.BlockSpec((n_groups, gdim, gdim), lambda i, g: (0, 0, 0)),
            pl.BlockSpec((1, d), lambda i, g: (0, 0)),
            pl.BlockSpec((1, d), lambda i, g: (0, 0)),
            pl.BlockSpec((1, d), lambda i, g: (0, 0)),
        ],
        out_specs=[
            pl.BlockSpec((ts, gdim), lambda i, g: (i, g)),
            pl.BlockSpec((ts, d), lambda i, g: (i, 0)),
        ],
        out_shape=[jax.ShapeDtypeStruct((m, d), F32), jax.ShapeDtypeStruct((m, d), BF16)],
        scratch_shapes=[
            pltpu.VMEM((n_groups, ts + POOL_HALO, gdim), F32),
            pltpu.VMEM((n_groups, ts, gdim), F32),
            pltpu.VMEM((2, ts + POOL_HALO, gdim), F32),
        ],
        compiler_params=_params(("arbitrary", "arbitrary"), 48),
        name="pool_layer",
    )(x2, x2, w_groups.astype(BF16), norm_gain.reshape(1, d), scale.reshape(1, d), next_gain.reshape(1, d))


def _ws_matmul(x, ws, extras, *, grid, x_spec, w_specs, extra_specs, out_shape, out_specs,
               epilogue, tn, recast, k_block=None, prefetch=(), valid=None, norm_gain=None, vmem_mib=48,
               name="ws_matmul"):
    if norm_gain is not None:
        assert k_block is None
        extras = list(extras) + [norm_gain.reshape(-1, 1)]
        extra_specs = list(extra_specs) + [pl.BlockSpec((x.shape[1], 1), lambda *_: (0, 0))]
    n_w, n_e, n_pf = len(ws), len(extras), len(prefetch)
    outs = out_shape if isinstance(out_shape, (list, tuple)) else [out_shape]
    n_out = len(outs)
    k = x.shape[1] if k_block is None else k_block
    cast_rows = _tile(k, 512, 16)

    def body(*refs):
        pf, refs = refs[:n_pf], refs[n_pf:]
        x_ref = refs[0]
        w_refs = refs[1:1 + n_w]
        e_refs = refs[1 + n_w:1 + n_w + n_e]
        o_refs = refs[1 + n_w + n_e:1 + n_w + n_e + n_out]
        wb_ref = refs[1 + n_w + n_e + n_out]
        n, m = pl.program_id(0), pl.program_id(1)

        if norm_gain is not None:
            e_refs, gain_ref = e_refs[:-1], e_refs[-1]

        def compute():
            @pl.when(recast(pf, n, m))
            def _():
                def chunk(c, carry):
                    r0 = pl.multiple_of(c * cast_rows, cast_rows)
                    for j, w_ref in enumerate(w_refs):
                        w = w_ref[pl.ds(r0, cast_rows), :]
                        if norm_gain is not None:
                            w = w * gain_ref[pl.ds(r0, cast_rows), :]
                        wb_ref[pl.ds(r0, cast_rows), j * tn:(j + 1) * tn] = w.astype(BF16)
                    return carry
                lax.fori_loop(0, k // cast_rows, chunk, 0)

            x = x_ref[...]
            acc = jnp.dot(x, wb_ref[...], preferred_element_type=F32)
            if norm_gain is not None:
                acc = acc * _rms_scale(x.astype(F32))
            epilogue(acc, e_refs, o_refs, pf, n, m)

        if valid is None:
            compute()
        else:
            ok = valid(pf, n, m)
            pl.when(ok)(compute)

            @pl.when(jnp.logical_not(ok))
            def _():
                for o_ref in o_refs:
                    o_ref[...] = jnp.zeros(o_ref.shape, o_ref.dtype)

    return pl.pallas_call(
        body,
        grid_spec=pltpu.PrefetchScalarGridSpec(
            num_scalar_prefetch=n_pf,
            grid=grid,
            in_specs=[x_spec] + list(w_specs) + list(extra_specs),
            out_specs=out_specs,
            scratch_shapes=[pltpu.VMEM((k, n_w * tn), BF16)],
        ),
        out_shape=out_shape,
        compiler_params=_params(("arbitrary", "arbitrary"), vmem_mib),
        name=name,
    )(*prefetch, x, *ws, *extras)


def _first_row_tile(pf, n, m):
    return m == 0


def _silu_mul(acc, tn):
    gate, up = acc[:, :tn], acc[:, tn:]
    return gate * jax.nn.sigmoid(gate) * up


def _gate_up(xn, w_gate_up):
    m, k = xn.shape
    dff = w_gate_up.shape[1] // 2
    tm, tn = _tile(m, 1024, 8), _tile(dff, 256, LANES)
    nt = dff // tn

    def epilogue(acc, e_refs, o_refs, pf, n, mi):
        o_refs[0][...] = _silu_mul(acc, tn).astype(BF16)

    return _ws_matmul(
        xn, [w_gate_up, w_gate_up], [],
        grid=(nt, m // tm),
        x_spec=pl.BlockSpec((tm, k), lambda n, mi: (mi, 0)),
        w_specs=[pl.BlockSpec((k, tn), lambda n, mi: (0, n)),
                 pl.BlockSpec((k, tn), lambda n, mi: (0, n + nt))],
        extra_specs=[],
        out_shape=jax.ShapeDtypeStruct((m, dff), BF16),
        out_specs=pl.BlockSpec((tm, tn), lambda n, mi: (mi, n)),
        epilogue=epilogue, tn=tn, recast=_first_row_tile, name="dense_gate_up")


MAX_WEIGHT_TILE_ROWS = 6144


def _store_with_bf16_copy(o_refs, val):
    o_refs[0][...] = val
    if len(o_refs) > 1:
        o_refs[1][...] = val.astype(BF16)


def _matmul_residual(x, w, res, *, name, bf16_copy=False):
    m, k = x.shape
    n_out = w.shape[1]
    n_chunks = pl.cdiv(k, MAX_WEIGHT_TILE_ROWS)
    assert k % (n_chunks * LANES) == 0
    kc = k // n_chunks
    tm, tn = _tile(m, 1024 if kc <= 4096 else 512, 8), _tile(n_out, 512, LANES)

    def epilogue(acc, e_refs, o_refs, pf, n, mi):
        _store_with_bf16_copy(o_refs, e_refs[0][...] + acc)

    tile = pl.BlockSpec((tm, tn), lambda n, mi: (mi, n))
    for c in range(n_chunks):
        dtypes = [F32, BF16] if bf16_copy and c == n_chunks - 1 else [F32]
        res = _ws_matmul(
            x, [w], [res],
            grid=(n_out // tn, m // tm),
            x_spec=pl.BlockSpec((tm, kc), lambda n, mi, c=c: (mi, c)),
            w_specs=[pl.BlockSpec((kc, tn), lambda n, mi, c=c: (c, n))],
            extra_specs=[tile],
            out_shape=[jax.ShapeDtypeStruct((m, n_out), dt) for dt in dtypes],
            out_specs=[tile] * len(dtypes),
            epilogue=epilogue, tn=tn, k_block=kc, recast=_first_row_tile, vmem_mib=56,
            name=name)
        res = res if bf16_copy and c == n_chunks - 1 else res[0]
    return res


def _ple(hn, w_gate, p_bf16, w_ple, h, layer, *, norm_gain=None, bf16_copy=False):
    m, k = hn.shape
    d = w_gate.shape[2]
    pdim = p_bf16.shape[2]
    tm, tn = _tile(m, 1024, 8), _tile(d, 512, LANES)

    def epilogue(acc, e_refs, o_refs, pf, n, mi):
        p_ref, wp_ref, h_ref = e_refs
        emb = jnp.dot(p_ref[...], wp_ref[...].astype(BF16), preferred_element_type=F32)
        _store_with_bf16_copy(o_refs, h_ref[...] + emb * jax.nn.sigmoid(acc))

    tile = pl.BlockSpec((tm, tn), lambda n, mi: (mi, n))
    dtypes = [F32, BF16] if bf16_copy else [F32]
    out = _ws_matmul(
        hn, [w_gate], [p_bf16, w_ple, h],
        grid=(d // tn, m // tm),
        x_spec=pl.BlockSpec((tm, k), lambda n, mi: (mi, 0)),
        w_specs=[pl.BlockSpec((None, k, tn), lambda n, mi: (layer, 0, n))],
        extra_specs=[pl.BlockSpec((None, tm, pdim), lambda n, mi: (layer, mi, 0)),
                     pl.BlockSpec((None, pdim, tn), lambda n, mi: (layer, 0, n)),
                     tile],
        out_shape=[jax.ShapeDtypeStruct((m, d), dt) for dt in dtypes],
        out_specs=[tile] * len(dtypes),
        epilogue=epilogue, tn=tn, recast=_first_row_tile, norm_gain=norm_gain, vmem_mib=56, name="ple")
    return out if bf16_copy else out[0]


def _rope_tables(seq, head_gain, post_scale):
    half = HEAD_DIM // 2
    inv_freq = jnp.exp(-math.log(ROPE_THETA) * jnp.arange(half, dtype=F32) / half)
    ang = jnp.arange(seq, dtype=F32)[:, None] * inv_freq[None, :]
    cos, sin = jnp.cos(ang), jnp.sin(ang)
    a = jnp.concatenate([cos, cos], axis=-1) * head_gain[None, :] * post_scale
    b = jnp.concatenate([-sin, sin], axis=-1) * jnp.roll(head_gain, half)[None, :] * post_scale
    return a, b


def _norm_rope_heads(acc, rope_a, rope_b, out_ref, tn):
    for hh in range(tn // HEAD_DIM):
        sl = slice(hh * HEAD_DIM, (hh + 1) * HEAD_DIM)
        y = acc[:, sl]
        y = (y * rope_a + pltpu.roll(y, HEAD_DIM // 2, axis=1) * rope_b) * _rms_scale(y)
        out_ref[:, sl] = y.astype(out_ref.dtype)


def _qk_projection(xn, w, rope_a, rope_b, seq, *, n_rope_cols, norm_gain, name):
    m, k = xn.shape
    n_out = w.shape[1]
    tm, tn = _tile(seq, 1024, 8), _tile(min(n_out, n_rope_cols), 512, LANES)
    n_rope_tiles = n_rope_cols // tn
    spb = seq // tm

    def epilogue(acc, e_refs, o_refs, pf, n, mi):
        a_ref, b_ref = e_refs

        @pl.when(n < n_rope_tiles)
        def _():
            _norm_rope_heads(acc, a_ref[...], b_ref[...], o_refs[0], tn)

        if n_rope_tiles * tn < n_out:
            @pl.when(n >= n_rope_tiles)
            def _():
                o_refs[0][...] = acc.astype(BF16)

    return _ws_matmul(
        xn, [w], [rope_a, rope_b],
        grid=(n_out // tn, m // tm),
        x_spec=pl.BlockSpec((tm, k), lambda n, mi: (mi, 0)),
        w_specs=[pl.BlockSpec((k, tn), lambda n, mi: (0, n))],
        extra_specs=[pl.BlockSpec((tm, HEAD_DIM), lambda n, mi: (mi % spb, 0)),
                     pl.BlockSpec((tm, HEAD_DIM), lambda n, mi: (mi % spb, 0))],
        out_shape=jax.ShapeDtypeStruct((m, n_out), BF16),
        out_specs=pl.BlockSpec((tm, tn), lambda n, mi: (mi, n)),
        epilogue=epilogue, tn=tn, recast=_first_row_tile, norm_gain=norm_gain, vmem_mib=56, name=name)


def _attn_body(q_ref, k_ref, v_ref, o_ref, qs_ref, ks_ref, vs_ref, acc_ref, m_ref, l_ref,
               s_ref, p_ref, a_ref, *, seq):
    blk = ATTN_BLOCK
    regroup = max(d for _, d in DILATED_BRANCHES)
    res_rows = seq // regroup
    row = lax.broadcasted_iota(jnp.int32, (blk, blk), 0)
    col = lax.broadcasted_iota(jnp.int32, (blk, blk), 1)
    contract_last = (((1,), (1,)), ((), ()))
    order = sorted(DILATED_BRANCHES, key=lambda wd: -wd[1])
    assert order[-1][1] == 1 and regroup % 8 == 0 and blk % regroup == 0

    def seq_pos(idx, pieces):
        n = blk // pieces
        return pieces * (idx % n) + idx // n

    def pieces_of(d, c, i):
        n = blk * d // regroup
        return [((c + d * a) * res_rows + n * i, n) for a in range(regroup // d)]

    def load(ref, pieces):
        parts = [ref[pl.ds(s, n), :] for s, n in pieces]
        return parts[0] if len(parts) == 1 else jnp.concatenate(parts, axis=0)

    def store(ref, pieces, val):
        off = 0
        for s, n in pieces:
            ref[pl.ds(s, n), :] = val[off:off + n]
            off += n

    gather_rows = (col == seq_pos(row, regroup)).astype(BF16)
    scatter_rows = (row == seq_pos(col, regroup)).astype(BF16)
    for i in range(seq // blk):
        for src, dst in ((q_ref, qs_ref), (k_ref, ks_ref), (v_ref, vs_ref)):
            regrouped = jnp.dot(gather_rows, src[pl.ds(i * blk, blk), :], preferred_element_type=F32)
            store(dst, pieces_of(1, 0, i), regrouped)

    for step, (window, d) in enumerate(order):
        first, last = step == 0, step == len(order) - 1
        pieces = regroup // d
        pos_r, pos_c = seq_pos(row, pieces), seq_pos(col, pieces)
        bias_cur = jnp.where(pos_c <= pos_r, 0.0, NEG).astype(F32)
        bias_prev = jnp.where(pos_c >= pos_r, 0.0, NEG).astype(F32)
        tiles = [(pieces_of(d, c, i), pieces_of(d, c, i - 1) if i > 0 else None, i)
                 for c in range(d) for i in range(seq // d // blk)]

        for t, (cur, prev, i) in enumerate(tiles):
            q = load(qs_ref, cur).astype(BF16)
            s_ref[t, :, :blk] = lax.dot_general(q, load(ks_ref, cur).astype(BF16), contract_last,
                                                preferred_element_type=F32) + bias_cur
            if prev is not None:
                s_ref[t, :, blk:] = lax.dot_general(q, load(ks_ref, prev).astype(BF16), contract_last,
                                                    preferred_element_type=F32) + bias_prev

        for t, (cur, prev, i) in enumerate(tiles):
            s_c = s_ref[t, :, :blk]
            mx = jnp.max(s_c, axis=-1, keepdims=True)
            if prev is not None:
                s_p = s_ref[t, :, blk:]
                mx = jnp.maximum(mx, jnp.max(s_p, axis=-1, keepdims=True))
            if first:
                m_new = jnp.broadcast_to(mx, (blk, LANES))
            else:
                m_old = load(m_ref, cur)
                m_new = jnp.maximum(m_old, mx)
            p_c = jnp.exp(s_c - m_new)
            den = jnp.sum(p_c, axis=-1, keepdims=True)
            p_ref[t, :, :blk] = p_c.astype(BF16)
            if prev is not None:
                p_p = jnp.exp(s_p - m_new)
                den = den + jnp.sum(p_p, axis=-1, keepdims=True)
                p_ref[t, :, blk:] = p_p.astype(BF16)
            if first:
                den = jnp.broadcast_to(den, (blk, LANES))
            else:
                alpha = jnp.exp(m_old - m_new)
                a_ref[t] = alpha
                den = alpha * load(l_ref, cur) + den
            store(l_ref, cur, den)
            if not last:
                store(m_ref, cur, m_new)

        for t, (cur, prev, i) in enumerate(tiles):
            acc = jnp.dot(p_ref[t, :, :blk], load(vs_ref, cur).astype(BF16), preferred_element_type=F32)
            if prev is not None:
                acc = acc + jnp.dot(p_ref[t, :, blk:], load(vs_ref, prev).astype(BF16),
                                    preferred_element_type=F32)
            if not first:
                acc = a_ref[t] * load(acc_ref, cur) + acc
            if last:
                out = jnp.dot(scatter_rows, (acc / load(l_ref, cur)).astype(BF16),
                              preferred_element_type=F32)
                o_ref[pl.ds(i * blk, blk), :] = out.astype(o_ref.dtype)
            else:
                store(acc_ref, cur, acc)


def _dilated_attention(q, kv, *, batch, seq):
    m, width = q.shape
    assert ATTN_BLOCK == LANES
    for window, d in DILATED_BRANCHES:
        assert window // d == ATTN_BLOCK and seq % (d * ATTN_BLOCK) == 0
    n_heads = width // HEAD_DIM
    head_block = pl.BlockSpec((seq, HEAD_DIM), lambda b, h: (b, h))
    return pl.pallas_call(
        functools.partial(_attn_body, seq=seq),
        grid=(batch, n_heads),
        in_specs=[head_block, head_block, pl.BlockSpec((seq, HEAD_DIM), lambda b, h: (b, n_heads + h))],
        out_specs=head_block,
        out_shape=jax.ShapeDtypeStruct((m, width), BF16),
        scratch_shapes=[pltpu.VMEM((seq, LANES), F32)] * 6 + [
            pltpu.VMEM((seq // ATTN_BLOCK, ATTN_BLOCK, 2 * ATTN_BLOCK), F32),
            pltpu.VMEM((seq // ATTN_BLOCK, ATTN_BLOCK, 2 * ATTN_BLOCK), BF16),
            pltpu.VMEM((seq // ATTN_BLOCK, ATTN_BLOCK, LANES), F32)],
        compiler_params=_params(("arbitrary", "arbitrary"), 32),
        name="dilated_attention",
    )(q, kv, kv)


def _router_body(h_ref, rg_ref, idx_ref, gate_ref):
    x = h_ref[...]
    inv = _rms_scale(x)
    lane = lax.broadcasted_iota(jnp.int32, (x.shape[0], LANES), 1).astype(F32)
    logits = jnp.full((x.shape[0], LANES), NEG, F32)
    for e in range(N_EXPERTS):
        logit = jnp.sum(x * rg_ref[e:e + 1, :], axis=-1, keepdims=True) * inv
        logits = jnp.where(lane == e, logit, logits)
    m1 = jnp.max(logits, axis=-1, keepdims=True)
    i1 = jnp.min(jnp.where(logits == m1, lane, float(LANES)), axis=-1, keepdims=True)
    rest = jnp.where(lane == i1, NEG, logits)
    m2 = jnp.max(rest, axis=-1, keepdims=True)
    i2 = jnp.min(jnp.where(rest == m2, lane, float(LANES)), axis=-1, keepdims=True)
    e2 = jnp.exp(m2 - m1)
    g1 = 1.0 / (1.0 + e2)
    g2 = e2 / (1.0 + e2)
    idx_ref[...] = jnp.where(lane == 0, i1, jnp.where(lane == 1, i2, 0.0)).astype(jnp.int32)
    gate_ref[...] = jnp.where(lane == 0, g1, jnp.where(lane == 1, g2, 0.0))


def _router(h, gain, router):
    m, d = h.shape
    n_e = router.shape[1]
    assert n_e == N_EXPERTS
    tm = _tile(m, 256, 8)
    gained_router = router.T * gain[None, :]
    row = pl.BlockSpec((tm, d), lambda i: (i, 0))
    small = pl.BlockSpec((tm, LANES), lambda i: (i, 0))
    return pl.pallas_call(
        _router_body,
        grid=(m // tm,),
        in_specs=[row, pl.BlockSpec((n_e, d), lambda i: (0, 0))],
        out_specs=[small, small],
        out_shape=[jax.ShapeDtypeStruct((m, LANES), jnp.int32), jax.ShapeDtypeStruct((m, LANES), F32)],
        compiler_params=_params(("arbitrary",), 32),
        name="router_top2",
    )(h, gained_router)


def _dispatch_plan(top_idx, tm):
    m = top_idx.shape[0]
    n_tiles = (m * TOP_K + N_EXPERTS * (tm - 1)) // tm
    flat_e = top_idx.reshape(-1)
    onehot = (flat_e[:, None] == jnp.arange(N_EXPERTS)[None, :]).astype(jnp.int32)
    rank = jnp.sum((jnp.cumsum(onehot, axis=0) - 1) * onehot, axis=1)
    counts = jnp.sum(onehot, axis=0)
    tiles_per = (counts + tm - 1) // tm
    tile_end = jnp.cumsum(tiles_per)
    row_start = (tile_end - tiles_per) * tm
    dest = row_start[flat_e] + rank
    src = jnp.zeros((n_tiles * tm,), jnp.int32).at[dest].set(jnp.arange(m * TOP_K, dtype=jnp.int32) // TOP_K)
    n_used = tile_end[-1]
    t = jnp.minimum(jnp.arange(n_tiles, dtype=jnp.int32), n_used - 1)
    tile_expert = jnp.sum((t[:, None] >= tile_end[None, :]).astype(jnp.int32), axis=1)
    tile_first = (t == (tile_end - tiles_per)[tile_expert]).astype(jnp.int32)
    tile_valid = (jnp.arange(n_tiles) < n_used).astype(jnp.int32)
    return (src, dest.reshape(m, TOP_K).astype(jnp.int32), tile_expert.astype(jnp.int32), t,
            tile_first, tile_valid, (n_used * tm).astype(jnp.int32).reshape(1))


DMA_LOOP_UNROLL = 8


def _row_dma_loop(n, copy_of, action):
    def step(j, c):
        getattr(copy_of(j), action)()
        return c
    lax.fori_loop(0, n, step, 0, unroll=DMA_LOOP_UNROLL)


DMA_CHUNKS = 8


def _pipelined_row_tiles(i, cur_used, next_used, copy, n_copies, consume, skip):
    per = n_copies // DMA_CHUNKS

    @pl.when((i == 0) & cur_used)
    def _():
        _row_dma_loop(n_copies, lambda q: copy(0, 0, q), "start")

    if skip is not None:
        pl.when(jnp.logical_not(cur_used))(skip)

    for s in range(2):
        for prefetch in (True, False):
            @pl.when(cur_used & (i % 2 == s) & (next_used if prefetch else jnp.logical_not(next_used)))
            def _():
                for c in range(DMA_CHUNKS):
                    if prefetch:
                        for q in range(c * per, (c + 1) * per):
                            copy(i + 1, 1 - s, q).start()
                    for q in range(c * per, (c + 1) * per):
                        copy(i, s, q).wait()
                    consume(s, c)


def _gather_body(src_ref, nrows_ref, gain_ref, h_hbm, o_ref, buf0_ref, buf1_ref, sems, *, rows):
    i = pl.program_id(0)
    bufs = (buf0_ref, buf1_ref)
    per = rows // DMA_CHUNKS

    def tile_used(t):
        return t * rows < nrows_ref[0]

    def copy(t, s, q):
        return pltpu.make_async_copy(h_hbm.at[pl.ds(src_ref[t * rows + q], 1)], bufs[s].at[pl.ds(q, 1)],
                                     sems.at[s, q // per])

    def consume(s, c):
        r = pl.ds(c * per, per)
        x = bufs[s][r, :]
        o_ref[r, :] = (x * _rms_scale(x) * gain_ref[...]).astype(BF16)

    def skip():
        o_ref[...] = jnp.zeros(o_ref.shape, o_ref.dtype)

    _pipelined_row_tiles(i, tile_used(i), (i + 1 < pl.num_programs(0)) & tile_used(i + 1), copy, rows,
                         consume, skip)


def _gather_norm_rows(h, gain, src, n_rows_used, rows):
    p = src.shape[0]
    d = h.shape[1]
    return pl.pallas_call(
        functools.partial(_gather_body, rows=rows),
        grid_spec=pltpu.PrefetchScalarGridSpec(
            num_scalar_prefetch=2,
            grid=(p // rows,),
            in_specs=[pl.BlockSpec((1, d), lambda i, src, nr: (0, 0)), pl.BlockSpec(memory_space=pl.ANY)],
            out_specs=pl.BlockSpec((rows, d), lambda i, src, nr: (i, 0)),
            scratch_shapes=[pltpu.VMEM((rows, d), F32), pltpu.VMEM((rows, d), F32),
                            pltpu.SemaphoreType.DMA((2, DMA_CHUNKS))],
        ),
        out_shape=jax.ShapeDtypeStruct((p, d), BF16),
        compiler_params=_params(("arbitrary",), 40),
        name="moe_gather_rows",
    )(src, n_rows_used, gain.reshape(1, d), h)


def _expert_changed(pf, n, m):
    return pf[2][m] == 1


def _tile_valid(pf, n, m):
    return pf[3][m] == 1


def _moe_gate_up(xs, w_gate_up, plan, tm):
    p, k = xs.shape
    dff = w_gate_up.shape[2] // 2
    tn = _tile(dff, 512, LANES)
    nt = dff // tn

    def epilogue(acc, e_refs, o_refs, pf, n, mi):
        o_refs[0][...] = _silu_mul(acc, tn).astype(BF16)

    return _ws_matmul(
        xs, [w_gate_up, w_gate_up], [],
        grid=(nt, p // tm),
        x_spec=pl.BlockSpec((tm, k), lambda n, mi, te, tr, tf, tv: (tr[mi], 0)),
        w_specs=[pl.BlockSpec((None, k, tn), lambda n, mi, te, tr, tf, tv: (te[mi], 0, n)),
                 pl.BlockSpec((None, k, tn), lambda n, mi, te, tr, tf, tv: (te[mi], 0, n + nt))],
        extra_specs=[],
        out_shape=jax.ShapeDtypeStruct((p, dff), BF16),
        out_specs=pl.BlockSpec((tm, tn), lambda n, mi, te, tr, tf, tv: (mi, n)),
        epilogue=epilogue, tn=tn, prefetch=plan, recast=_expert_changed, valid=_tile_valid,
        vmem_mib=56, name="moe_gate_up")


def _moe_down(hs, w_down, plan, tm):
    p, k = hs.shape
    d = w_down.shape[2]
    tn = _tile(d, 1024, LANES)

    def epilogue(acc, e_refs, o_refs, pf, n, mi):
        o_refs[0][...] = acc

    return _ws_matmul(
        hs, [w_down], [],
        grid=(d // tn, p // tm),
        x_spec=pl.BlockSpec((tm, k), lambda n, mi, te, tr, tf, tv: (tr[mi], 0)),
        w_specs=[pl.BlockSpec((None, k, tn), lambda n, mi, te, tr, tf, tv: (te[mi], 0, n))],
        extra_specs=[],
        out_shape=jax.ShapeDtypeStruct((p, d), F32),
        out_specs=pl.BlockSpec((tm, tn), lambda n, mi, te, tr, tf, tv: (mi, n)),
        epilogue=epilogue, tn=tn, prefetch=plan, recast=_expert_changed, valid=_tile_valid,
        vmem_mib=56, name="moe_down")


def _combine_body(pos_ref, h_ref, gate_ref, gain_ref, ys_hbm, h_out_ref, hn_ref, buf0_ref, buf1_ref, sems,
                  *, tm):
    i = pl.program_id(0)
    bufs = (buf0_ref, buf1_ref)
    n_copies = tm * TOP_K
    per = n_copies // DMA_CHUNKS
    tokens_per = per // TOP_K

    def copy(t, s, q):
        return pltpu.make_async_copy(ys_hbm.at[pl.ds(pos_ref[t * n_copies + q], 1)],
                                     bufs[s].at[q % TOP_K, pl.ds(q // TOP_K, 1)], sems.at[s, q // per])

    def consume(s, c):
        r = pl.ds(c * tokens_per, tokens_per)
        h = h_ref[r, :]
        gates = gate_ref[r, :]
        for k in range(TOP_K):
            h = h + gates[:, k:k + 1] * bufs[s][k, r, :]
        h_out_ref[r, :] = h
        hn_ref[r, :] = (h * _rms_scale(h) * gain_ref[...]).astype(BF16)

    _pipelined_row_tiles(i, i >= 0, i + 1 < pl.num_programs(0), copy, n_copies, consume, None)


def _moe_combine(h, ys, pos, gates, next_gain):
    m, d = h.shape
    tm = _tile(m, 256, 8)
    row = pl.BlockSpec((tm, d), lambda i, pos: (i, 0))
    return pl.pallas_call(
        functools.partial(_combine_body, tm=tm),
        grid_spec=pltpu.PrefetchScalarGridSpec(
            num_scalar_prefetch=1,
            grid=(m // tm,),
            in_specs=[row, pl.BlockSpec((tm, LANES), lambda i, pos: (i, 0)),
                      pl.BlockSpec((1, d), lambda i, pos: (0, 0)), pl.BlockSpec(memory_space=pl.ANY)],
            out_specs=[row, row],
            scratch_shapes=[pltpu.VMEM((TOP_K, tm, d), F32), pltpu.VMEM((TOP_K, tm, d), F32),
                            pltpu.SemaphoreType.DMA((2, DMA_CHUNKS))],
        ),
        out_shape=[jax.ShapeDtypeStruct((m, d), F32), jax.ShapeDtypeStruct((m, d), BF16)],
        compiler_params=_params(("arbitrary",), 48),
        name="moe_combine",
    )(pos.reshape(-1), h, gates, next_gain.reshape(1, d), ys)


def kernel(x, p, pool_norm, pool_w, pool_scale, kv_norm, w_kv, k_norm, attn_norm, w_q, q_norm, w_o,
           ffn_norm, dense_w_gate_up, dense_w_down, moe_router, moe_w_gate_up, moe_w_down, ple_w,
           ple_norm, ple_gate_w):
    batch, seq, d = x.shape
    m = batch * seq
    p_bf16 = p.astype(BF16).reshape(p.shape[0], m, p.shape[-1])

    h, hn = _pool_layer(x.reshape(m, d), seq, pool_norm[0], pool_w[0], pool_scale[0], ffn_norm[0])
    act = _gate_up(hn, dense_w_gate_up[0])
    h, h_bf16 = _matmul_residual(act, dense_w_down[0], h, name="dense_down", bf16_copy=True)
    h, h_bf16 = _ple(h_bf16, ple_gate_w, p_bf16, ple_w, h, 0, norm_gain=ple_norm[0], bf16_copy=True)

    width = w_q.shape[2]
    kv = _qk_projection(h_bf16, w_kv, *_rope_tables(seq, k_norm, 1.0), seq, n_rope_cols=width,
                        norm_gain=kv_norm, name="kv_proj")
    q = _qk_projection(h_bf16, w_q[0], *_rope_tables(seq, q_norm[0], 1.0 / math.sqrt(HEAD_DIM)), seq,
                       n_rope_cols=width, norm_gain=attn_norm[0], name="q_proj")

    attn = _dilated_attention(q, kv, batch=batch, seq=seq)
    h = _matmul_residual(attn, w_o[0], h, name="attn_out_proj")

    top_idx, gates = _router(h, ffn_norm[1], moe_router[0])
    tm_moe = _tile(m, 512, 8)
    src, pos, tile_expert, tile_row, tile_first, tile_valid, n_rows_used = _dispatch_plan(
        top_idx[:, :TOP_K], tm_moe)
    plan = (tile_expert, tile_row, tile_first, tile_valid)
    xs = _gather_norm_rows(h, ffn_norm[1], src, n_rows_used, tm_moe)
    hs = _moe_gate_up(xs, moe_w_gate_up[0], plan, tm_moe)
    ys = _moe_down(hs, moe_w_down[0], plan, tm_moe)
    h, hn = _moe_combine(h, ys, pos, gates, ple_norm[1])
    h = _ple(hn, ple_gate_w, p_bf16, ple_w, h, 1)
    return h.reshape(batch, seq, d)
```

```python
import functools
import math

import jax
import jax.numpy as jnp
from jax import lax
from jax.experimental import pallas as pl
from jax.experimental.pallas import tpu as pltpu

F32 = jnp.float32
BF16 = jnp.bfloat16

HEAD_DIM = 128
LANES = 128
ROPE_THETA = 10000.0
RMS_EPS = 1e-6
POOL_WINDOWS = (2, 4, 8, 16)
POOL_HALO = 32
DILATED_BRANCHES = ((128, 1), (512, 4), (2048, 16))
ATTN_BLOCK = 128
N_EXPERTS = 8
TOP_K = 2
NEG = -1e30
MIB = 1024 * 1024


def _params(semantics, vmem_mib):
    return pltpu.CompilerParams(dimension_semantics=semantics, vmem_limit_bytes=vmem_mib * MIB)


def _tile(dim, pref, quantum):
    if dim <= pref:
        return dim
    t = (pref // quantum) * quantum
    while t > quantum and dim % t:
        t -= quantum
    assert dim % t == 0, (dim, pref, quantum)
    return t


def _rms_scale(x):
    return lax.rsqrt(jnp.mean(x * x, axis=-1, keepdims=True) + RMS_EPS)


def _pool_body(x_ref, halo_ref, w_ref, gain_ref, scale_ref, next_gain_ref,
               h_ref, hn_ref, yn_ref, hrow_ref, lvl_ref, *, ts, seq, n_groups, gdim):
    i = pl.program_id(0)
    g = pl.program_id(1)
    t0 = (i * ts) % seq

    @pl.when(g == 0)
    def _():
        x = x_ref[...]
        y = x * _rms_scale(x) * gain_ref[...]
        xh = halo_ref[...]
        yh = xh * _rms_scale(xh) * gain_ref[...]
        yh = yh * (t0 > 0).astype(F32)
        for gi in range(n_groups):
            yn_ref[gi, :POOL_HALO, :] = yh[:, gi * gdim:(gi + 1) * gdim]
            yn_ref[gi, POOL_HALO:, :] = y[:, gi * gdim:(gi + 1) * gdim]

    pos = t0 + lax.broadcasted_iota(jnp.int32, (ts, 1), 0)
    for gi, win in enumerate(POOL_WINDOWS):
        @pl.when(g == gi)
        def _():
            levels = win.bit_length() - 1
            assert win == 1 << levels and 8 * levels <= POOL_HALO
            src = yn_ref.at[gi]
            for lv in range(1, levels + 1):
                shift, lo = 1 << (lv - 1), 8 * lv
                dst = lvl_ref.at[lv % 2]
                dst[lo:, :] = src[lo:, :] + src[lo - shift:ts + POOL_HALO - shift, :]
                src = dst
            cur = yn_ref[gi, POOL_HALO:, :]
            acc = src[POOL_HALO:, :]
            inv_cnt = 1.0 / jnp.minimum(pos + 1, win).astype(F32)
            pooled = (acc * inv_cnt - cur).astype(BF16)
            sl = slice(gi * gdim, (gi + 1) * gdim)
            mixed = jnp.dot(pooled, w_ref[gi], preferred_element_type=F32)
            hrow_ref[gi] = x_ref[:, sl] + mixed * scale_ref[:, sl]

    h_ref[...] = hrow_ref[g]

    @pl.when(g == n_groups - 1)
    def _():
        ss = jnp.zeros((ts, 1), F32)
        for gi in range(n_groups):
            hg = hrow_ref[gi]
            ss = ss + jnp.sum(hg * hg, axis=-1, keepdims=True)
        inv = lax.rsqrt(ss / (n_groups * gdim) + RMS_EPS)
        for gi in range(n_groups):
            sl = slice(gi * gdim, (gi + 1) * gdim)
            hn_ref[:, sl] = (hrow_ref[gi] * inv * next_gain_ref[:, sl]).astype(BF16)


def _pool_layer(x2, seq, norm_gain, w_groups, scale, next_gain):
    m, d = x2.shape
    n_groups, gdim = w_groups.shape[0], w_groups.shape[1]
    assert n_groups == len(POOL_WINDOWS) and n_groups * gdim == d
    ts = _tile(seq, 256, POOL_HALO)
    hpt = ts // POOL_HALO
    body = functools.partial(_pool_body, ts=ts, seq=seq, n_groups=n_groups, gdim=gdim)
    return pl.pallas_call(
        body,
        grid=(m // ts, n_groups),
        in_specs=[
            pl.BlockSpec((ts, d), lambda i, g: (i, 0)),
            pl.BlockSpec((POOL_HALO, d), lambda i, g: (jnp.maximum(i * hpt - 1, 0), 0)),
            pl.BlockSpec((n_groups, gdim, gdim), lambda i, g: (0, 0, 0)),
            pl.BlockSpec((1, d), lambda i, g: (0, 0)),
            pl.BlockSpec((1, d), lambda i, g: (0, 0)),
            pl.BlockSpec((1, d), lambda i, g: (0, 0)),
        ],
        out_specs=[
            pl.BlockSpec((ts, gdim), lambda i, g: (i, g)),
            pl.BlockSpec((ts, d), lambda i, g: (i, 0)),
        ],
        out_shape=[jax.ShapeDtypeStruct((m, d), F32), jax.ShapeDtypeStruct((m, d), BF16)],
        scratch_shapes=[
            pltpu.VMEM((n_groups, ts + POOL_HALO, gdim), F32),
            pltpu.VMEM((n_groups, ts, gdim), F32),
            pltpu.VMEM((2, ts + POOL_HALO, gdim), F32),
        ],
        compiler_params=_params(("arbitrary", "arbitrary"), 48),
        name="pool_layer",
    )(x2, x2, w_groups.astype(BF16), norm_gain.reshape(1, d), scale.reshape(1, d), next_gain.reshape(1, d))


def _ws_matmul(x, ws, extras, *, grid, x_spec, w_specs, extra_specs, out_shape, out_specs,
               epilogue, tn, recast, k_block=None, prefetch=(), valid=None, norm_gain=None, vmem_mib=48,
               name="ws_matmul"):
    if norm_gain is not None:
        assert k_block is None
        extras = list(extras) + [norm_gain.reshape(-1, 1)]
        extra_specs = list(extra_specs) + [pl.BlockSpec((x.shape[1], 1), lambda *_: (0, 0))]
    n_w, n_e, n_pf = len(ws), len(extras), len(prefetch)
    outs = out_shape if isinstance(out_shape, (list, tuple)) else [out_shape]
    n_out = len(outs)
    k = x.shape[1] if k_block is None else k_block
    cast_rows = _tile(k, 512, 16)

    def body(*refs):
        pf, refs = refs[:n_pf], refs[n_pf:]
        x_ref = refs[0]
        w_refs = refs[1:1 + n_w]
        e_refs = refs[1 + n_w:1 + n_w + n_e]
        o_refs = refs[1 + n_w + n_e:1 + n_w + n_e + n_out]
        wb_ref = refs[1 + n_w + n_e + n_out]
        n, m = pl.program_id(0), pl.program_id(1)

        if norm_gain is not None:
            e_refs, gain_ref = e_refs[:-1], e_refs[-1]

        def compute():
            @pl.when(recast(pf, n, m))
            def _():
                def chunk(c, carry):
                    r0 = pl.multiple_of(c * cast_rows, cast_rows)
                    for j, w_ref in enumerate(w_refs):
                        w = w_ref[pl.ds(r0, cast_rows), :]
                        if norm_gain is not None:
                            w = w * gain_ref[pl.ds(r0, cast_rows), :]
                        wb_ref[pl.ds(r0, cast_rows), j * tn:(j + 1) * tn] = w.astype(BF16)
                    return carry
                lax.fori_loop(0, k // cast_rows, chunk, 0)

            x = x_ref[...]
            acc = jnp.dot(x, wb_ref[...], preferred_element_type=F32)
            if norm_gain is not None:
                acc = acc * _rms_scale(x.astype(F32))
            epilogue(acc, e_refs, o_refs, pf, n, m)

        if valid is None:
            compute()
        else:
            ok = valid(pf, n, m)
            pl.when(ok)(compute)

            @pl.when(jnp.logical_not(ok))
            def _():
                for o_ref in o_refs:
                    o_ref[...] = jnp.zeros(o_ref.shape, o_ref.dtype)

    return pl.pallas_call(
        body,
        grid_spec=pltpu.PrefetchScalarGridSpec(
            num_scalar_prefetch=n_pf,
            grid=grid,
            in_specs=[x_spec] + list(w_specs) + list(extra_specs),
            out_specs=out_specs,
            scratch_shapes=[pltpu.VMEM((k, n_w * tn), BF16)],
        ),
        out_shape=out_shape,
        compiler_params=_params(("arbitrary", "arbitrary"), vmem_mib),
        name=name,
    )(*prefetch, x, *ws, *extras)


def _first_row_tile(pf, n, m):
    return m == 0


def _silu_mul(acc, tn):
    gate, up = acc[:, :tn], acc[:, tn:]
    return gate * jax.nn.sigmoid(gate) * up


def _gate_up(xn, w_gate_up):
    m, k = xn.shape
    dff = w_gate_up.shape[1] // 2
    tm, tn = _tile(m, 1024, 8), _tile(dff, 256, LANES)
    nt = dff // tn

    def epilogue(acc, e_refs, o_refs, pf, n, mi):
        o_refs[0][...] = _silu_mul(acc, tn).astype(BF16)

    return _ws_matmul(
        xn, [w_gate_up, w_gate_up], [],
        grid=(nt, m // tm),
        x_spec=pl.BlockSpec((tm, k), lambda n, mi: (mi, 0)),
        w_specs=[pl.BlockSpec((k, tn), lambda n, mi: (0, n)),
                 pl.BlockSpec((k, tn), lambda n, mi: (0, n + nt))],
        extra_specs=[],
        out_shape=jax.ShapeDtypeStruct((m, dff), BF16),
        out_specs=pl.BlockSpec((tm, tn), lambda n, mi: (mi, n)),
        epilogue=epilogue, tn=tn, recast=_first_row_tile, name="dense_gate_up")


MAX_WEIGHT_TILE_ROWS = 6144


def _store_with_bf16_copy(o_refs, val):
    o_refs[0][...] = val
    if len(o_refs) > 1:
        o_refs[1][...] = val.astype(BF16)


def _matmul_residual(x, w, res, *, name, bf16_copy=False):
    m, k = x.shape
    n_out = w.shape[1]
    n_chunks = pl.cdiv(k, MAX_WEIGHT_TILE_ROWS)
    assert k % (n_chunks * LANES) == 0
    kc = k // n_chunks
    tm, tn = _tile(m, 1024 if kc <= 4096 else 512, 8), _tile(n_out, 512, LANES)

    def epilogue(acc, e_refs, o_refs, pf, n, mi):
        _store_with_bf16_copy(o_refs, e_refs[0][...] + acc)

    tile = pl.BlockSpec((tm, tn), lambda n, mi: (mi, n))
    for c in range(n_chunks):
        dtypes = [F32, BF16] if bf16_copy and c == n_chunks - 1 else [F32]
        res = _ws_matmul(
            x, [w], [res],
            grid=(n_out // tn, m // tm),
            x_spec=pl.BlockSpec((tm, kc), lambda n, mi, c=c: (mi, c)),
            w_specs=[pl.BlockSpec((kc, tn), lambda n, mi, c=c: (c, n))],
            extra_specs=[tile],
            out_shape=[jax.ShapeDtypeStruct((m, n_out), dt) for dt in dtypes],
            out_specs=[tile] * len(dtypes),
            epilogue=epilogue, tn=tn, k_block=kc, recast=_first_row_tile, vmem_mib=56,
            name=name)
        res = res if bf16_copy and c == n_chunks - 1 else res[0]
    return res


def _ple(hn, w_gate, p_bf16, w_ple, h, layer, *, norm_gain=None, bf16_copy=False):
    m, k = hn.shape
    d = w_gate.shape[2]
    pdim = p_bf16.shape[2]
    tm, tn = _tile(m, 1024, 8), _tile(d, 512, LANES)

    def epilogue(acc, e_refs, o_refs, pf, n, mi):
        p_ref, wp_ref, h_ref = e_refs
        emb = jnp.dot(p_ref[...], wp_ref[...].astype(BF16), preferred_element_type=F32)
        _store_with_bf16_copy(o_refs, h_ref[...] + emb * jax.nn.sigmoid(acc))

    tile = pl.BlockSpec((tm, tn), lambda n, mi: (mi, n))
    dtypes = [F32, BF16] if bf16_copy else [F32]
    out = _ws_matmul(
        hn, [w_gate], [p_bf16, w_ple, h],
        grid=(d // tn, m // tm),
        x_spec=pl.BlockSpec((tm, k), lambda n, mi: (mi, 0)),
        w_specs=[pl.BlockSpec((None, k, tn), lambda n, mi: (layer, 0, n))],
        extra_specs=[pl.BlockSpec((None, tm, pdim), lambda n, mi: (layer, mi, 0)),
                     pl.BlockSpec((None, pdim, tn), lambda n, mi: (layer, 0, n)),
                     tile],
        out_shape=[jax.ShapeDtypeStruct((m, d), dt) for dt in dtypes],
        out_specs=[tile] * len(dtypes),
        epilogue=epilogue, tn=tn, recast=_first_row_tile, norm_gain=norm_gain, vmem_mib=56, name="ple")
    return out if bf16_copy else out[0]


def _rope_tables(seq, head_gain, post_scale):
    half = HEAD_DIM // 2
    inv_freq = jnp.exp(-math.log(ROPE_THETA) * jnp.arange(half, dtype=F32) / half)
    ang = jnp.arange(seq, dtype=F32)[:, None] * inv_freq[None, :]
    cos, sin = jnp.cos(ang), jnp.sin(ang)
    a = jnp.concatenate([cos, cos], axis=-1) * head_gain[None, :] * post_scale
    b = jnp.concatenate([-sin, sin], axis=-1) * jnp.roll(head_gain, half)[None, :] * post_scale
    return a, b


def _norm_rope_heads(acc, rope_a, rope_b, out_ref, tn):
    for hh in range(tn // HEAD_DIM):
        sl = slice(hh * HEAD_DIM, (hh + 1) * HEAD_DIM)
        y = acc[:, sl]
        y = (y * rope_a + pltpu.roll(y, HEAD_DIM // 2, axis=1) * rope_b) * _rms_scale(y)
        out_ref[:, sl] = y.astype(out_ref.dtype)


def _qk_projection(xn, w, rope_a, rope_b, seq, *, n_rope_cols, norm_gain, name):
    m, k = xn.shape
    n_out = w.shape[1]
    tm, tn = _tile(seq, 1024, 8), _tile(min(n_out, n_rope_cols), 512, LANES)
    n_rope_tiles = n_rope_cols // tn
    spb = seq // tm

    def epilogue(acc, e_refs, o_refs, pf, n, mi):
        a_ref, b_ref = e_refs

        @pl.when(n < n_rope_tiles)
        def _():
            _norm_rope_heads(acc, a_ref[...], b_ref[...], o_refs[0], tn)

        if n_rope_tiles * tn < n_out:
            @pl.when(n >= n_rope_tiles)
            def _():
                o_refs[0][...] = acc.astype(BF16)

    return _ws_matmul(
        xn, [w], [rope_a, rope_b],
        grid=(n_out // tn, m // tm),
        x_spec=pl.BlockSpec((tm, k), lambda n, mi: (mi, 0)),
        w_specs=[pl.BlockSpec((k, tn), lambda n, mi: (0, n))],
        extra_specs=[pl.BlockSpec((tm, HEAD_DIM), lambda n, mi: (mi % spb, 0)),
                     pl.BlockSpec((tm, HEAD_DIM), lambda n, mi: (mi % spb, 0))],
        out_shape=jax.ShapeDtypeStruct((m, n_out), BF16),
        out_specs=pl.BlockSpec((tm, tn), lambda n, mi: (mi, n)),
        epilogue=epilogue, tn=tn, recast=_first_row_tile, norm_gain=norm_gain, vmem_mib=56, name=name)


def _attn_body(q_ref, k_ref, v_ref, o_ref, qs_ref, ks_ref, vs_ref, acc_ref, m_ref, l_ref,
               s_ref, p_ref, a_ref, *, seq):
    blk = ATTN_BLOCK
    regroup = max(d for _, d in DILATED_BRANCHES)
    res_rows = seq // regroup
    row = lax.broadcasted_iota(jnp.int32, (blk, blk), 0)
    col = lax.broadcasted_iota(jnp.int32, (blk, blk), 1)
    contract_last = (((1,), (1,)), ((), ()))
    order = sorted(DILATED_BRANCHES, key=lambda wd: -wd[1])
    assert order[-1][1] == 1 and regroup % 8 == 0 and blk % regroup == 0

    def seq_pos(idx, pieces):
        n = blk // pieces
        return pieces * (idx % n) + idx // n

    def pieces_of(d, c, i):
        n = blk * d // regroup
        return [((c + d * a) * res_rows + n * i, n) for a in range(regroup // d)]

    def load(ref, pieces):
        parts = [ref[pl.ds(s, n), :] for s, n in pieces]
        return parts[0] if len(parts) == 1 else jnp.concatenate(parts, axis=0)

    def store(ref, pieces, val):
        off = 0
        for s, n in pieces:
            ref[pl.ds(s, n), :] = val[off:off + n]
            off += n

    gather_rows = (col == seq_pos(row, regroup)).astype(BF16)
    scatter_rows = (row == seq_pos(col, regroup)).astype(BF16)
    for i in range(seq // blk):
        for src, dst in ((q_ref, qs_ref), (k_ref, ks_ref), (v_ref, vs_ref)):
            regrouped = jnp.dot(gather_rows, src[pl.ds(i * blk, blk), :], preferred_element_type=F32)
            store(dst, pieces_of(1, 0, i), regrouped)

    for step, (window, d) in enumerate(order):
        first, last = step == 0, step == len(order) - 1
        pieces = regroup // d
        pos_r, pos_c = seq_pos(row, pieces), seq_pos(col, pieces)
        bias_cur = jnp.where(pos_c <= pos_r, 0.0, NEG).astype(F32)
        bias_prev = jnp.where(pos_c >= pos_r, 0.0, NEG).astype(F32)
        tiles = [(pieces_of(d, c, i), pieces_of(d, c, i - 1) if i > 0 else None, i)
                 for c in range(d) for i in range(seq // d // blk)]

        for t, (cur, prev, i) in enumerate(tiles):
            q = load(qs_ref, cur).astype(BF16)
            s_ref[t, :, :blk] = lax.dot_general(q, load(ks_ref, cur).astype(BF16), contract_last,
                                                preferred_element_type=F32) + bias_cur
            if prev is not None:
                s_ref[t, :, blk:] = lax.dot_general(q, load(ks_ref, prev).astype(BF16), contract_last,
                                                    preferred_element_type=F32) + bias_prev

        for t, (cur, prev, i) in enumerate(tiles):
            s_c = s_ref[t, :, :blk]
            mx = jnp.max(s_c, axis=-1, keepdims=True)
            if prev is not None:
                s_p = s_ref[t, :, blk:]
                mx = jnp.maximum(mx, jnp.max(s_p, axis=-1, keepdims=True))
            if first:
                m_new = jnp.broadcast_to(mx, (blk, LANES))
            else:
                m_old = load(m_ref, cur)
                m_new = jnp.maximum(m_old, mx)
            p_c = jnp.exp(s_c - m_new)
            den = jnp.sum(p_c, axis=-1, keepdims=True)
            p_ref[t, :, :blk] = p_c.astype(BF16)
            if prev is not None:
                p_p = jnp.exp(s_p - m_new)
                den = den + jnp.sum(p_p, axis=-1, keepdims=True)
                p_ref[t, :, blk:] = p_p.astype(BF16)
            if first:
                den = jnp.broadcast_to(den, (blk, LANES))
            else:
                alpha = jnp.exp(m_old - m_new)
                a_ref[t] = alpha
                den = alpha * load(l_ref, cur) + den
            store(l_ref, cur, den)
            if not last:
                store(m_ref, cur, m_new)

        for t, (cur, prev, i) in enumerate(tiles):
            acc = jnp.dot(p_ref[t, :, :blk], load(vs_ref, cur).astype(BF16), preferred_element_type=F32)
            if prev is not None:
                acc = acc + jnp.dot(p_ref[t, :, blk:], load(vs_ref, prev).astype(BF16),
                                    preferred_element_type=F32)
            if not first:
                acc = a_ref[t] * load(acc_ref, cur) + acc
            if last:
                out = jnp.dot(scatter_rows, (acc / load(l_ref, cur)).astype(BF16),
                              preferred_element_type=F32)
                o_ref[pl.ds(i * blk, blk), :] = out.astype(o_ref.dtype)
            else:
                store(acc_ref, cur, acc)


def _dilated_attention(q, kv, *, batch, seq):
    m, width = q.shape
    assert ATTN_BLOCK == LANES
    for window, d in DILATED_BRANCHES:
        assert window // d == ATTN_BLOCK and seq % (d * ATTN_BLOCK) == 0
    n_heads = width // HEAD_DIM
    head_block = pl.BlockSpec((seq, HEAD_DIM), lambda b, h: (b, h))
    return pl.pallas_call(
        functools.partial(_attn_body, seq=seq),
        grid=(batch, n_heads),
        in_specs=[head_block, head_block, pl.BlockSpec((seq, HEAD_DIM), lambda b, h: (b, n_heads + h))],
        out_specs=head_block,
        out_shape=jax.ShapeDtypeStruct((m, width), BF16),
        scratch_shapes=[pltpu.VMEM((seq, LANES), F32)] * 6 + [
            pltpu.VMEM((seq // ATTN_BLOCK, ATTN_BLOCK, 2 * ATTN_BLOCK), F32),
            pltpu.VMEM((seq // ATTN_BLOCK, ATTN_BLOCK, 2 * ATTN_BLOCK), BF16),
            pltpu.VMEM((seq // ATTN_BLOCK, ATTN_BLOCK, LANES), F32)],
        compiler_params=_params(("arbitrary", "arbitrary"), 32),
        name="dilated_attention",
    )(q, kv, kv)


def _router_body(h_ref, rg_ref, idx_ref, gate_ref):
    x = h_ref[...]
    inv = _rms_scale(x)
    lane = lax.broadcasted_iota(jnp.int32, (x.shape[0], LANES), 1).astype(F32)
    logits = jnp.full((x.shape[0], LANES), NEG, F32)
    for e in range(N_EXPERTS):
        logit = jnp.sum(x * rg_ref[e:e + 1, :], axis=-1, keepdims=True) * inv
        logits = jnp.where(lane == e, logit, logits)
    m1 = jnp.max(logits, axis=-1, keepdims=True)
    i1 = jnp.min(jnp.where(logits == m1, lane, float(LANES)), axis=-1, keepdims=True)
    rest = jnp.where(lane == i1, NEG, logits)
    m2 = jnp.max(rest, axis=-1, keepdims=True)
    i2 = jnp.min(jnp.where(rest == m2, lane, float(LANES)), axis=-1, keepdims=True)
    e2 = jnp.exp(m2 - m1)
    g1 = 1.0 / (1.0 + e2)
    g2 = e2 / (1.0 + e2)
    idx_ref[...] = jnp.where(lane == 0, i1, jnp.where(lane == 1, i2, 0.0)).astype(jnp.int32)
    gate_ref[...] = jnp.where(lane == 0, g1, jnp.where(lane == 1, g2, 0.0))


def _router(h, gain, router):
    m, d = h.shape
    n_e = router.shape[1]
    assert n_e == N_EXPERTS
    tm = _tile(m, 256, 8)
    gained_router = router.T * gain[None, :]
    row = pl.BlockSpec((tm, d), lambda i: (i, 0))
    small = pl.BlockSpec((tm, LANES), lambda i: (i, 0))
    return pl.pallas_call(
        _router_body,
        grid=(m // tm,),
        in_specs=[row, pl.BlockSpec((n_e, d), lambda i: (0, 0))],
        out_specs=[small, small],
        out_shape=[jax.ShapeDtypeStruct((m, LANES), jnp.int32), jax.ShapeDtypeStruct((m, LANES), F32)],
        compiler_params=_params(("arbitrary",), 32),
        name="router_top2",
    )(h, gained_router)


def _dispatch_plan(top_idx, tm):
    m = top_idx.shape[0]
    n_tiles = (m * TOP_K + N_EXPERTS * (tm - 1)) // tm
    flat_e = top_idx.reshape(-1)
    onehot = (flat_e[:, None] == jnp.arange(N_EXPERTS)[None, :]).astype(jnp.int32)
    rank = jnp.sum((jnp.cumsum(onehot, axis=0) - 1) * onehot, axis=1)
    counts = jnp.sum(onehot, axis=0)
    tiles_per = (counts + tm - 1) // tm
    tile_end = jnp.cumsum(tiles_per)
    row_start = (tile_end - tiles_per) * tm
    dest = row_start[flat_e] + rank
    src = jnp.zeros((n_tiles * tm,), jnp.int32).at[dest].set(jnp.arange(m * TOP_K, dtype=jnp.int32) // TOP_K)
    n_used = tile_end[-1]
    t = jnp.minimum(jnp.arange(n_tiles, dtype=jnp.int32), n_used - 1)
    tile_expert = jnp.sum((t[:, None] >= tile_end[None, :]).astype(jnp.int32), axis=1)
    tile_first = (t == (tile_end - tiles_per)[tile_expert]).astype(jnp.int32)
    tile_valid = (jnp.arange(n_tiles) < n_used).astype(jnp.int32)
    return (src, dest.reshape(m, TOP_K).astype(jnp.int32), tile_expert.astype(jnp.int32), t,
            tile_first, tile_valid, (n_used * tm).astype(jnp.int32).reshape(1))


DMA_LOOP_UNROLL = 8


def _row_dma_loop(n, copy_of, action):
    def step(j, c):
        getattr(copy_of(j), action)()
        return c
    lax.fori_loop(0, n, step, 0, unroll=DMA_LOOP_UNROLL)


DMA_CHUNKS = 8


def _pipelined_row_tiles(i, cur_used, next_used, copy, n_copies, consume, skip):
    per = n_copies // DMA_CHUNKS

    @pl.when((i == 0) & cur_used)
    def _():
        _row_dma_loop(n_copies, lambda q: copy(0, 0, q), "start")

    if skip is not None:
        pl.when(jnp.logical_not(cur_used))(skip)

    for s in range(2):
        for prefetch in (True, False):
            @pl.when(cur_used & (i % 2 == s) & (next_used if prefetch else jnp.logical_not(next_used)))
            def _():
                for c in range(DMA_CHUNKS):
                    if prefetch:
                        for q in range(c * per, (c + 1) * per):
                            copy(i + 1, 1 - s, q).start(priority=q % 2)
                    for q in range(c * per, (c + 1) * per):
                        copy(i, s, q).wait()
                    consume(s, c)


def _gather_body(src_ref, nrows_ref, gain_ref, h_hbm, o_ref, buf0_ref, buf1_ref, sems, *, rows):
    i = pl.program_id(0)
    bufs = (buf0_ref, buf1_ref)
    per = rows // DMA_CHUNKS

    def tile_used(t):
        return t * rows < nrows_ref[0]

    def copy(t, s, q):
        return pltpu.make_async_copy(h_hbm.at[pl.ds(src_ref[t * rows + q], 1)], bufs[s].at[pl.ds(q, 1)],
                                     sems.at[s, q // per])

    def consume(s, c):
        r = pl.ds(c * per, per)
        x = bufs[s][r, :]
        o_ref[r, :] = (x * _rms_scale(x) * gain_ref[...]).astype(BF16)

    def skip():
        o_ref[...] = jnp.zeros(o_ref.shape, o_ref.dtype)

    _pipelined_row_tiles(i, tile_used(i), (i + 1 < pl.num_programs(0)) & tile_used(i + 1), copy, rows,
                         consume, skip)


def _gather_norm_rows(h, gain, src, n_rows_used, rows):
    p = src.shape[0]
    d = h.shape[1]
    return pl.pallas_call(
        functools.partial(_gather_body, rows=rows),
        grid_spec=pltpu.PrefetchScalarGridSpec(
            num_scalar_prefetch=2,
            grid=(p // rows,),
            in_specs=[pl.BlockSpec((1, d), lambda i, src, nr: (0, 0)), pl.BlockSpec(memory_space=pl.ANY)],
            out_specs=pl.BlockSpec((rows, d), lambda i, src, nr: (i, 0)),
            scratch_shapes=[pltpu.VMEM((rows, d), F32), pltpu.VMEM((rows, d), F32),
                            pltpu.SemaphoreType.DMA((2, DMA_CHUNKS))],
        ),
        out_shape=jax.ShapeDtypeStruct((p, d), BF16),
        compiler_params=_params(("arbitrary",), 40),
        name="moe_gather_rows",
    )(src, n_rows_used, gain.reshape(1, d), h)


def _expert_changed(pf, n, m):
    return pf[2][m] == 1


def _tile_valid(pf, n, m):
    return pf[3][m] == 1


def _moe_gate_up(xs, w_gate_up, plan, tm):
    p, k = xs.shape
    dff = w_gate_up.shape[2] // 2
    tn = _tile(dff, 512, LANES)
    nt = dff // tn

    def epilogue(acc, e_refs, o_refs, pf, n, mi):
        o_refs[0][...] = _silu_mul(acc, tn).astype(BF16)

    return _ws_matmul(
        xs, [w_gate_up, w_gate_up], [],
        grid=(nt, p // tm),
        x_spec=pl.BlockSpec((tm, k), lambda n, mi, te, tr, tf, tv: (tr[mi], 0)),
        w_specs=[pl.BlockSpec((None, k, tn), lambda n, mi, te, tr, tf, tv: (te[mi], 0, n)),
                 pl.BlockSpec((None, k, tn), lambda n, mi, te, tr, tf, tv: (te[mi], 0, n + nt))],
        extra_specs=[],
        out_shape=jax.ShapeDtypeStruct((p, dff), BF16),
        out_specs=pl.BlockSpec((tm, tn), lambda n, mi, te, tr, tf, tv: (mi, n)),
        epilogue=epilogue, tn=tn, prefetch=plan, recast=_expert_changed, valid=_tile_valid,
        vmem_mib=56, name="moe_gate_up")


def _moe_down(hs, w_down, plan, tm):
    p, k = hs.shape
    d = w_down.shape[2]
    tn = _tile(d, 1024, LANES)

    def epilogue(acc, e_refs, o_refs, pf, n, mi):
        o_refs[0][...] = acc

    return _ws_matmul(
        hs, [w_down], [],
        grid=(d // tn, p // tm),
        x_spec=pl.BlockSpec((tm, k), lambda n, mi, te, tr, tf, tv: (tr[mi], 0)),
        w_specs=[pl.BlockSpec((None, k, tn), lambda n, mi, te, tr, tf, tv: (te[mi], 0, n))],
        extra_specs=[],
        out_shape=jax.ShapeDtypeStruct((p, d), F32),
        out_specs=pl.BlockSpec((tm, tn), lambda n, mi, te, tr, tf, tv: (mi, n)),
        epilogue=epilogue, tn=tn, prefetch=plan, recast=_expert_changed, valid=_tile_valid,
        vmem_mib=56, name="moe_down")


def _combine_body(pos_ref, h_ref, gate_ref, gain_ref, ys_hbm, h_out_ref, hn_ref, buf0_ref, buf1_ref, sems,
                  *, tm):
    i = pl.program_id(0)
    bufs = (buf0_ref, buf1_ref)
    n_copies = tm * TOP_K
    per = n_copies // DMA_CHUNKS
    tokens_per = per // TOP_K

    def copy(t, s, q):
        return pltpu.make_async_copy(ys_hbm.at[pl.ds(pos_ref[t * n_copies + q], 1)],
                                     bufs[s].at[q % TOP_K, pl.ds(q // TOP_K, 1)], sems.at[s, q // per])

    def consume(s, c):
        r = pl.ds(c * tokens_per, tokens_per)
        h = h_ref[r, :]
        gates = gate_ref[r, :]
        for k in range(TOP_K):
            h = h + gates[:, k:k + 1] * bufs[s][k, r, :]
        h_out_ref[r, :] = h
        hn_ref[r, :] = (h * _rms_scale(h) * gain_ref[...]).astype(BF16)

    _pipelined_row_tiles(i, i >= 0, i + 1 < pl.num_programs(0), copy, n_copies, consume, None)


def _moe_combine(h, ys, pos, gates, next_gain):
    m, d = h.shape
    tm = _tile(m, 256, 8)
    row = pl.BlockSpec((tm, d), lambda i, pos: (i, 0))
    return pl.pallas_call(
        functools.partial(_combine_body, tm=tm),
        grid_spec=pltpu.PrefetchScalarGridSpec(
            num_scalar_prefetch=1,
            grid=(m // tm,),
            in_specs=[row, pl.BlockSpec((tm, LANES), lambda i, pos: (i, 0)),
                      pl.BlockSpec((1, d), lambda i, pos: (0, 0)), pl.BlockSpec(memory_space=pl.ANY)],
            out_specs=[row, row],
            scratch_shapes=[pltpu.VMEM((TOP_K, tm, d), F32), pltpu.VMEM((TOP_K, tm, d), F32),
                            pltpu.SemaphoreType.DMA((2, DMA_CHUNKS))],
        ),
        out_shape=[jax.ShapeDtypeStruct((m, d), F32), jax.ShapeDtypeStruct((m, d), BF16)],
        compiler_params=_params(("arbitrary",), 48),
        name="moe_combine",
    )(pos.reshape(-1), h, gates, next_gain.reshape(1, d), ys)


def kernel(x, p, pool_norm, pool_w, pool_scale, kv_norm, w_kv, k_norm, attn_norm, w_q, q_norm, w_o,
           ffn_norm, dense_w_gate_up, dense_w_down, moe_router, moe_w_gate_up, moe_w_down, ple_w,
           ple_norm, ple_gate_w):
    batch, seq, d = x.shape
    m = batch * seq
    p_bf16 = p.astype(BF16).reshape(p.shape[0], m, p.shape[-1])

    h, hn = _pool_layer(x.reshape(m, d), seq, pool_norm[0], pool_w[0], pool_scale[0], ffn_norm[0])
    act = _gate_up(hn, dense_w_gate_up[0])
    h, h_bf16 = _matmul_residual(act, dense_w_down[0], h, name="dense_down", bf16_copy=True)
    h, h_bf16 = _ple(h_bf16, ple_gate_w, p_bf16, ple_w, h, 0, norm_gain=ple_norm[0], bf16_copy=True)

    width = w_q.shape[2]
    kv = _qk_projection(h_bf16, w_kv, *_rope_tables(seq, k_norm, 1.0), seq, n_rope_cols=width,
                        norm_gain=kv_norm, name="kv_proj")
    q = _qk_projection(h_bf16, w_q[0], *_rope_tables(seq, q_norm[0], 1.0 / math.sqrt(HEAD_DIM)), seq,
                       n_rope_cols=width, norm_gain=attn_norm[0], name="q_proj")

    attn = _dilated_attention(q, kv, batch=batch, seq=seq)
    h = _matmul_residual(attn, w_o[0], h, name="attn_out_proj")

    top_idx, gates = _router(h, ffn_norm[1], moe_router[0])
    tm_moe = _tile(m, 512, 8)
    src, pos, tile_expert, tile_row, tile_first, tile_valid, n_rows_used = _dispatch_plan(
        top_idx[:, :TOP_K], tm_moe)
    plan = (tile_expert, tile_row, tile_first, tile_valid)
    xs = _gather_norm_rows(h, ffn_norm[1], src, n_rows_used, tm_moe)
    hs = _moe_gate_up(xs, moe_w_gate_up[0], plan, tm_moe)
    ys = _moe_down(hs, moe_w_down[0], plan, tm_moe)
    h, hn = _moe_combine(h, ys, pos, gates, ple_norm[1])
    h = _ple(hn, ple_gate_w, p_bf16, ple_w, h, 1)
    return h.reshape(batch, seq, d)
```
